```python
import math
import jax, jax.numpy as jnp
from jax import lax
import numpy as np

D_MODEL = 1024
BATCH = 4
SEQ = 4096
DEPTH = 1

HEAD_DIM = 64
DIL_GROUPS = ((128, 1), (512, 4), (2048, 16))
N_DIL_GROUPS = 3
A_HEADS_PER_GROUP = 4
A_HEADS = N_DIL_GROUPS * A_HEADS_PER_GROUP
A_WIDTH = A_HEADS * HEAD_DIM
A_OUT = A_HEADS_PER_GROUP * HEAD_DIM
SB_HEADS = 8
SB_WIDTH = SB_HEADS * HEAD_DIM
BLOCK = 128
ROPE_THETA = 10000.0
MEM_LEN = 256
X_HEADS = 4
X_HEAD_DIM = 128
X_WIDTH = X_HEADS * X_HEAD_DIM
D_FF = ((8 * D_MODEL // 3 + 127) // 128) * 128
CONV_WIDTH = 3
EPS = 1e-6
IN_COLS = 3 * A_WIDTH + 3 * SB_WIDTH + 2 * D_MODEL

kernel_name = "hybrid_dilated_stickbreak_gated_block"


def rms_norm(x, g):
    xf = x.astype(jnp.float32)
    y = xf * lax.rsqrt(jnp.mean(xf * xf, axis=-1, keepdims=True) + EPS)
    return (y * g.astype(jnp.float32)).astype(x.dtype)


def rope(x, pos):
    dh = x.shape[-1]
    half = dh // 2
    inv_freq = 1.0 / (ROPE_THETA ** (jnp.arange(half, dtype=jnp.float32) * (2.0 / dh)))
    ang = pos.astype(jnp.float32)[:, None] * inv_freq[None, :]
    cos = jnp.cos(ang)[None, :, None, :]
    sin = jnp.sin(ang)[None, :, None, :]
    xf = x.astype(jnp.float32)
    x1, x2 = xf[..., :half], xf[..., half:]
    out = jnp.concatenate([x1 * cos - x2 * sin, x2 * cos + x1 * sin], axis=-1)
    return out.astype(x.dtype)


def local_window_attn(q, k, v, n_back):
    n, L, dh = q.shape
    nb = -(-L // BLOCK)
    lp = nb * BLOCK
    pad = lp - L
    qp = jnp.pad(q.astype(jnp.float32), ((0, 0), (0, pad), (0, 0)))
    kp = jnp.pad(k.astype(jnp.float32), ((0, 0), (BLOCK, pad), (0, 0)))
    vp = jnp.pad(v.astype(jnp.float32), ((0, 0), (BLOCK, pad), (0, 0)))
    qb = qp.reshape(n, nb, BLOCK, dh)
    kb = jnp.concatenate([kp[:, :lp].reshape(n, nb, BLOCK, dh),
                          kp[:, BLOCK:].reshape(n, nb, BLOCK, dh)], axis=2)
    vb = jnp.concatenate([vp[:, :lp].reshape(n, nb, BLOCK, dh),
                          vp[:, BLOCK:].reshape(n, nb, BLOCK, dh)], axis=2)
    s = jnp.einsum('nbqd,nbkd->nbqk', qb, kb) * (dh ** -0.5)
    a_idx = jnp.arange(BLOCK)[:, None]
    c_idx = jnp.arange(2 * BLOCK)[None, :]
    dist = a_idx - c_idx + BLOCK
    kpos = jnp.arange(nb)[:, None, None] * BLOCK + c_idx[None] - BLOCK
    mask = ((dist >= 0) & (dist <= n_back))[None] & (kpos >= 0)
    s = jnp.where(mask[None], s, -jnp.inf)
    lse = jax.nn.logsumexp(s, axis=-1)
    p = jnp.exp(s - lse[..., None])
    o = jnp.einsum('nbqk,nbkd->nbqd', p, vb)
    return o.reshape(n, lp, dh)[:, :L], lse.reshape(n, lp)[:, :L]


def dilated_group(q, k, v, dilation, n_back):
    b, s, h, dh = q.shape
    L = s // dilation

    def split(t):
        return t.reshape(b, L, dilation, h, dh).transpose(0, 3, 2, 1, 4).reshape(b * h * dilation, L, dh)

    o, lse = local_window_attn(split(q), split(k), split(v), n_back)
    o = o.reshape(b, h, dilation, L, dh).transpose(0, 3, 2, 1, 4).reshape(b, s, h, dh)
    lse = lse.reshape(b, h, dilation, L).transpose(0, 3, 2, 1).reshape(b, s, h)
    return o, lse


def dilated_mixture(q, k, v):
    b, s = q.shape[0], q.shape[1]
    outs, lses = [], []
    for g, (window, dilation) in enumerate(DIL_GROUPS):
        sl = slice(g * A_HEADS_PER_GROUP, (g + 1) * A_HEADS_PER_GROUP)
        o, l = dilated_group(q[:, :, sl], k[:, :, sl], v[:, :, sl], dilation, window // dilation)
        outs.append(o)
        lses.append(l)
    o = jnp.stack(outs, axis=0)
    wts = jax.nn.softmax(jnp.stack(lses, axis=0), axis=0)
    y = jnp.sum(wts[..., None] * o, axis=0)
    return y.reshape(b, s, A_OUT)


def stick_breaking(q, k, v):
    b, s, h, dh = q.shape
    nb = s // BLOCK
    qf = q.astype(jnp.float32) * (dh ** -0.5)
    kf = k.astype(jnp.float32)
    vf = v.astype(jnp.float32)
    qb = qf.reshape(b, nb, BLOCK, h, dh).transpose(1, 0, 3, 2, 4)
    kpos = jnp.arange(s)

    def block(args):
        qblk, t0 = args
        z = jnp.einsum('bhqd,bshd->bhqs', qblk, kf)
        tq = t0 + jnp.arange(BLOCK)
        mask = kpos[None, :] < tq[:, None]
        log_keep = jnp.where(mask, jax.nn.log_sigmoid(-z), 0.0)
        after = lax.cumsum(log_keep, axis=3, reverse=True) - log_keep
        a = jnp.where(mask, jnp.exp(jax.nn.log_sigmoid(z) + after), 0.0)
        return jnp.einsum('bhqs,bshd->bhqd', a, vf)

    o = lax.map(block, (qb, jnp.arange(nb) * BLOCK))
    return o.transpose(1, 0, 3, 2, 4).reshape(b, s, h * dh)


def memory_cross_attn(n, m, w_xq, w_xkv, w_xo):
    b, s, _ = n.shape
    ml = m.shape[1]
    q = (n @ w_xq).reshape(b, s, X_HEADS, X_HEAD_DIM)
    kv = m @ w_xkv
    k = kv[..., :X_WIDTH].reshape(b, ml, X_HEADS, X_HEAD_DIM)
    v = kv[..., X_WIDTH:].reshape(b, ml, X_HEADS, X_HEAD_DIM)
    sc = jnp.einsum('bshd,bmhd->bhsm', q.astype(jnp.float32), k.astype(jnp.float32)) * (X_HEAD_DIM ** -0.5)
    p = jax.nn.softmax(sc, axis=-1)
    o = jnp.einsum('bhsm,bmhd->bshd', p, v.astype(jnp.float32)).reshape(b, s, X_WIDTH)
    return o.astype(n.dtype) @ w_xo


def conv_gated_mlp(n, w_up, conv_w, conv_b, w_down):
    u = n @ w_up
    c = u.shape[-1]
    u = lax.conv_general_dilated(
        u, conv_w.astype(u.dtype)[:, None, :], window_strides=(1,),
        padding=[(CONV_WIDTH - 1, 0)], dimension_numbers=('NWC', 'WIO', 'NWC'),
        feature_group_count=c) + conv_b.astype(u.dtype)
    gate, val = u[..., :D_FF], u[..., D_FF:]
    return (jax.nn.silu(gate) * val) @ w_down


def setup_inputs(seed: int = 0) -> dict:
    key = jax.random.key(seed)
    ks = jax.random.split(key, 20)
    f32 = jnp.float32

    def w(k, shape, fan_in):
        return jax.random.normal(k, shape, f32) * (fan_in ** -0.5)

    def gain(k, shape):
        return 1.0 + 0.02 * jax.random.normal(k, shape, f32)

    L = DEPTH
    return {
        "x": jax.random.normal(ks[0], (BATCH, SEQ, D_MODEL), f32),
        "mem": jax.random.normal(ks[1], (BATCH, MEM_LEN, D_MODEL), f32),
        "ln_mix_g": gain(ks[2], (L, D_MODEL)),
        "w_in": w(ks[3], (L, D_MODEL, IN_COLS), D_MODEL),
        "b_gate": 0.02 * jax.random.normal(ks[4], (L, 2 * D_MODEL), f32),
        "w_branch_a": w(ks[5], (L, A_OUT, D_MODEL), A_OUT),
        "w_branch_b": w(ks[6], (L, SB_WIDTH, D_MODEL), SB_WIDTH),
        "w_out": w(ks[7], (L, D_MODEL, D_MODEL), D_MODEL),
        "ln_x_g": gain(ks[8], (L, D_MODEL)),
        "ln_mem_g": gain(ks[9], (L, D_MODEL)),
        "w_xq": w(ks[10], (L, D_MODEL, X_WIDTH), D_MODEL),
        "w_xkv": w(ks[11], (L, D_MODEL, 2 * X_WIDTH), D_MODEL),
        "w_xo": w(ks[12], (L, X_WIDTH, D_MODEL), X_WIDTH),
        "ln_ffn_g": gain(ks[13], (L, D_MODEL)),
        "w_up": w(ks[14], (L, D_MODEL, 2 * D_FF), D_MODEL),
        "conv_w": w(ks[15], (L, CONV_WIDTH, 2 * D_FF), CONV_WIDTH),
        "conv_b": 0.02 * jax.random.normal(ks[16], (L, 2 * D_FF), f32),
        "w_down": w(ks[17], (L, D_FF, D_MODEL), D_FF),
        "ln_f_g": gain(ks[18], (D_MODEL,)),
    }


def reference(x, mem, ln_mix_g, w_in, b_gate, w_branch_a, w_branch_b, w_out,
              ln_x_g, ln_mem_g, w_xq, w_xkv, w_xo,
              ln_ffn_g, w_up, conv_w, conv_b, w_down, ln_f_g):
    b, s, _ = x.shape
    pos = jnp.arange(s)
    splits = [A_WIDTH, 2 * A_WIDTH, 3 * A_WIDTH,
              3 * A_WIDTH + SB_WIDTH, 3 * A_WIDTH + 2 * SB_WIDTH, 3 * A_WIDTH + 3 * SB_WIDTH,
              3 * A_WIDTH + 3 * SB_WIDTH + D_MODEL]
    h = x
    for l in range(DEPTH):
        n = rms_norm(h, ln_mix_g[l])
        proj = n @ w_in[l]
        qa, ka, va, qb, kb, vb, ga, gb = jnp.split(proj, splits, axis=-1)
        qa = rope(qa.reshape(b, s, A_HEADS, HEAD_DIM), pos)
        ka = rope(ka.reshape(b, s, A_HEADS, HEAD_DIM), pos)
        va = va.reshape(b, s, A_HEADS, HEAD_DIM)
        ya = dilated_mixture(qa, ka, va).astype(h.dtype) @ w_branch_a[l]
        yb = stick_breaking(qb.reshape(b, s, SB_HEADS, HEAD_DIM),
                            kb.reshape(b, s, SB_HEADS, HEAD_DIM),
                            vb.reshape(b, s, SB_HEADS, HEAD_DIM)).astype(h.dtype) @ w_branch_b[l]
        gate_a = jax.nn.sigmoid(ga + b_gate[l, :D_MODEL])
        gate_b = jax.nn.sigmoid(gb + b_gate[l, D_MODEL:])
        h = h + (gate_a * ya + gate_b * yb) @ w_out[l]
        n = rms_norm(h, ln_x_g[l])
        m = rms_norm(mem, ln_mem_g[l])
        h = h + memory_cross_attn(n, m, w_xq[l], w_xkv[l], w_xo[l])
        n = rms_norm(h, ln_ffn_g[l])
        h = h + conv_gated_mlp(n, w_up[l], conv_w[l], conv_b[l], w_down[l])
    return rms_norm(h, ln_f_g)
```

```python
import functools

import jax
import jax.numpy as jnp
from jax import lax
from jax.experimental import pallas as pl
from jax.experimental.pallas import tpu as pltpu

F32 = jnp.float32
BF16 = jnp.bfloat16

LANES = 128
SUBLANES = 8
VMEM_LIMIT_BYTES = 56 * 1024 * 1024

HEAD_DIM = 64
DIL_GROUPS = ((128, 1), (512, 4), (2048, 16))
A_HEADS_PER_GROUP = 4
A_GROUP_WIDTH = A_HEADS_PER_GROUP * HEAD_DIM
A_WIDTH = len(DIL_GROUPS) * A_GROUP_WIDTH
SB_HEADS = 8
SB_WIDTH = SB_HEADS * HEAD_DIM
BLOCK = 128
ROPE_THETA = 10000.0
X_HEADS = 4
X_HEAD_DIM = 128
X_WIDTH = X_HEADS * X_HEAD_DIM
CONV_WIDTH = 3
EPS = 1e-6

F32_EXP_ZERO_BELOW = -104.0
MASKED_SCORE = -1e30


def _rms_norm_bf16(x, g):
    ms = jnp.mean(x * x, axis=-1, keepdims=True)
    return (x * lax.rsqrt(ms + EPS) * g).astype(BF16)


def _dot(a, b):
    return jnp.dot(a, b, preferred_element_type=F32)


def _dot_nt(a, b):
    return lax.dot_general(a, b, (((1,), (1,)), ((), ())), preferred_element_type=F32)


def _params(*semantics):
    return pltpu.CompilerParams(dimension_semantics=semantics, vmem_limit_bytes=VMEM_LIMIT_BYTES)


def _mem_kv_kernel(m_ref, g_ref, w_ref, o_ref):
    n = _rms_norm_bf16(m_ref[0], g_ref[...])
    o_ref[0] = _dot(n, w_ref[...]).astype(BF16)


def _mem_kv(mem, g, w_bf16):
    b, ml, d = mem.shape
    n_out = w_bf16.shape[1]
    return pl.pallas_call(
        _mem_kv_kernel,
        grid=(b,),
        in_specs=[pl.BlockSpec((1, ml, d), lambda i: (i, 0, 0)),
                  pl.BlockSpec((1, d), lambda i: (0, 0)),
                  pl.BlockSpec((d, n_out), lambda i: (0, 0))],
        out_specs=pl.BlockSpec((1, ml, n_out), lambda i: (i, 0, 0)),
        out_shape=jax.ShapeDtypeStruct((b, ml, n_out), BF16),
        compiler_params=_params("arbitrary"),
        name="mem_kv",
    )(mem, g, w_bf16)


IN_PROJ_ROWS = 512


def _in_proj_kernel(x_ref, g_ref, w_ref, bg_ref, cos_ref, sin_ref,
                    qa0, qa1, qa2, ka0, ka1, ka2, va0, va1, va2, qb, kb, vb, ga, gb):
    n = _rms_norm_bf16(x_ref[...], g_ref[...])
    rows = n.shape[0]
    cos = cos_ref[...]
    sin = sin_ref[...]
    lane = lax.broadcasted_iota(jnp.int32, (rows, LANES), 1)
    first_half = (lane % HEAD_DIM) < (HEAD_DIM // 2)

    def rope(t):
        partner = jnp.where(first_half, pltpu.roll(t, LANES - HEAD_DIM // 2, 1),
                            pltpu.roll(t, HEAD_DIM // 2, 1))
        return t * cos + partner * sin

    def proj(c0, width):
        return _dot(n, w_ref[:, c0:c0 + width])

    q_scale = HEAD_DIM ** -0.5
    col = 0
    for ref in (qa0, qa1, qa2):
        t = proj(col, A_GROUP_WIDTH)
        for h in range(A_GROUP_WIDTH // LANES):
            sl = slice(h * LANES, (h + 1) * LANES)
            ref[:, sl] = (rope(t[:, sl]) * q_scale).astype(BF16)
        col += A_GROUP_WIDTH
    for ref in (ka0, ka1, ka2):
        t = proj(col, A_GROUP_WIDTH)
        for h in range(A_GROUP_WIDTH // LANES):
            sl = slice(h * LANES, (h + 1) * LANES)
            ref[:, sl] = rope(t[:, sl]).astype(BF16)
        col += A_GROUP_WIDTH
    for ref in (va0, va1, va2):
        ref[...] = proj(col, A_GROUP_WIDTH).astype(BF16)
        col += A_GROUP_WIDTH
    chunk = 256
    for ref, scale in ((qb, q_scale), (kb, None), (vb, None)):
        for c in range(SB_WIDTH // chunk):
            t = proj(col, chunk)
            if scale is not None:
                t = t * scale
            ref[:, c * chunk:(c + 1) * chunk] = t.astype(BF16)
            col += chunk
    d_model = ga.shape[1]
    gcol = 0
    for ref in (ga, gb):
        for c in range(d_model // chunk):
            t = proj(col, chunk) + bg_ref[:, gcol:gcol + chunk]
            ref[:, c * chunk:(c + 1) * chunk] = (1.0 / (1.0 + jnp.exp(-t))).astype(BF16)
            col += chunk
            gcol += chunk


def _in_proj(x2, g, w_bf16, b_gate, cos_t, sin_t, seq):
    t_rows, d = x2.shape
    tm = IN_PROJ_ROWS
    n_cols = w_bf16.shape[1]
    tiles_per_seq = seq // tm
    row_spec = lambda width: pl.BlockSpec((tm, width), lambda i: (i, 0))
    widths = [A_GROUP_WIDTH] * 9 + [SB_WIDTH] * 3 + [d] * 2
    return pl.pallas_call(
        _in_proj_kernel,
        grid=(t_rows // tm,),
        in_specs=[row_spec(d),
                  pl.BlockSpec((1, d), lambda i: (0, 0)),
                  pl.BlockSpec((d, n_cols), lambda i: (0, 0)),
                  pl.BlockSpec((1, 2 * d), lambda i: (0, 0)),
                  pl.BlockSpec((tm, LANES), lambda i: (i % tiles_per_seq, 0)),
                  pl.BlockSpec((tm, LANES), lambda i: (i % tiles_per_seq, 0))],
        out_specs=[row_spec(w) for w in widths],
        out_shape=[jax.ShapeDtypeStruct((t_rows, w), BF16) for w in widths],
        compiler_params=_params("arbitrary"),
        name="in_proj",
    )(x2, g, w_bf16, b_gate, cos_t, sin_t)


def _dil_attn_kernel(q_ref, kc_ref, kp_ref, vc_ref, vp_ref, o_ref, lse_ref, *, n_back):
    j = pl.program_id(2)
    rows = q_ref.shape[1]
    row = lax.broadcasted_iota(jnp.int32, (BLOCK, 2 * BLOCK), 0)
    col = lax.broadcasted_iota(jnp.int32, (BLOCK, 2 * BLOCK), 1)
    dist = row - col + BLOCK
    band = (dist >= 0) & (dist <= n_back)
    first_col = jnp.where(j > 0, 0, BLOCK)
    band_first = band & (col >= first_col)
    lane = lax.broadcasted_iota(jnp.int32, (BLOCK, LANES), 1)
    head0 = lane < HEAD_DIM

    for sub in range(rows // BLOCK):
        cur = slice(sub * BLOCK, (sub + 1) * BLOCK)
        q = q_ref[0, cur, :]
        if sub == 0:
            k2 = jnp.concatenate([kp_ref[0], kc_ref[0, cur, :]], axis=0)
            v2 = jnp.concatenate([vp_ref[0], vc_ref[0, cur, :]], axis=0)
            mask = band_first
        else:
            both = slice((sub - 1) * BLOCK, (sub + 1) * BLOCK)
            k2 = kc_ref[0, both, :]
            v2 = vc_ref[0, both, :]
            mask = band
        outs, lses = [], []
        for is_head0 in (True, False):
            sel = head0 if is_head0 else jnp.logical_not(head0)
            qh = jnp.where(sel, q, jnp.zeros_like(q))
            s = jnp.where(mask, _dot_nt(qh, k2), MASKED_SCORE)
            m = jnp.max(s, axis=1, keepdims=True)
            p = jnp.exp(s - m)
            l = jnp.sum(p, axis=1, keepdims=True)
            outs.append(_dot(p.astype(BF16), v2) * (1.0 / l))
            lses.append(m + jnp.log(l))
        o_ref[0, cur, :] = jnp.where(head0, outs[0], outs[1]).astype(o_ref.dtype)
        lse_ref[0, cur, :] = jnp.where(head0, lses[0], lses[1])


def _dil_attn(q, k, v, dilation, n_back, name):
    assert n_back <= BLOCK
    b, s, w = q.shape
    length = s // dilation
    rows = min(length, 4 * BLOCK)
    blocks_per_chunk = rows // BLOCK
    view = lambda t: t.reshape(b, length, dilation * w)
    n_col_blocks = dilation * w // LANES
    cur_spec = pl.BlockSpec((1, rows, LANES), lambda bi, c, j: (bi, j, c))
    prev_spec = pl.BlockSpec((1, BLOCK, LANES),
                             lambda bi, c, j: (bi, jnp.maximum(j * blocks_per_chunk - 1, 0), c))
    o, lse = pl.pallas_call(
        functools.partial(_dil_attn_kernel, n_back=n_back),
        grid=(b, n_col_blocks, length // rows),
        in_specs=[cur_spec, cur_spec, prev_spec, cur_spec, prev_spec],
        out_specs=[cur_spec, cur_spec],
        out_shape=[jax.ShapeDtypeStruct((b, length, dilation * w), BF16),
                   jax.ShapeDtypeStruct((b, length, dilation * w), F32)],
        compiler_params=_params("arbitrary", "arbitrary", "arbitrary"),
        name=name,
    )(view(q), view(k), view(k), view(v), view(v))
    return o.reshape(b * s, w), lse.reshape(b * s, w)


def _suffix_sum_matrix():
    j = jnp.arange(2 * BLOCK)[:, None]
    c = jnp.arange(4 * BLOCK)[None, :]
    same_head = (j // BLOCK) == ((c % (2 * BLOCK)) // BLOCK)
    later = (j % BLOCK) > (c % BLOCK)
    r = jnp.where(c < 2 * BLOCK, same_head & later, same_head)
    return r.astype(BF16)


def _stickbreak_kernel(q_ref, k_ref, v_ref, r_ref, o_ref, c_ref, acc_ref):
    qi = pl.program_id(1)
    n_pairs = q_ref.shape[2] // LANES
    c_ref[...] = jnp.zeros_like(c_ref)
    acc_ref[...] = jnp.zeros_like(acc_ref)
    lane = lax.broadcasted_iota(jnp.int32, (BLOCK, LANES), 1)
    head0 = lane < HEAD_DIM
    row = lax.broadcasted_iota(jnp.int32, (BLOCK, 2 * BLOCK), 0)
    key = lax.broadcasted_iota(jnp.int32, (BLOCK, 2 * BLOCK), 1) % BLOCK

    def split_heads(t):
        zero = jnp.zeros_like(t)
        return jnp.concatenate([jnp.where(head0, t, zero), jnp.where(head0, zero, t)], axis=0)

    def body(carry):
        kb, _ = carry
        k0 = pl.multiple_of(kb * BLOCK, BLOCK)
        causal = (k0 + key) < (qi * BLOCK + row)
        for p in range(n_pairs):
            cols = slice(p * LANES, (p + 1) * LANES)
            ccols = slice(p * 2 * BLOCK, (p + 1) * 2 * BLOCK)
            q = q_ref[0, :, cols]
            kst = split_heads(k_ref[0, pl.ds(k0, BLOCK), cols])
            vst = split_heads(v_ref[0, pl.ds(k0, BLOCK), cols])
            z = _dot_nt(q, kst)
            log_keep_all = -(jnp.maximum(z, 0.0) + jnp.log1p(jnp.exp(-jnp.abs(z))))
            log_keep = jnp.where(causal, log_keep_all, 0.0)
            hi = log_keep.astype(BF16)
            lo = (log_keep - hi.astype(F32)).astype(BF16)
            sums = _dot(jnp.concatenate([hi, lo], axis=0), r_ref[...])
            sums = sums[:BLOCK] + sums[BLOCK:]
            c_old = c_ref[:, ccols]
            after = c_old + sums[:, :2 * BLOCK]
            a = jnp.where(causal, jnp.exp(log_keep_all + z + after), 0.0)
            acc_ref[:, cols] += _dot(a.astype(BF16), vst)
            c_ref[:, ccols] = c_old + sums[:, 2 * BLOCK:]
        all_dead = jnp.max(c_ref[...]) < F32_EXP_ZERO_BELOW
        done = jnp.logical_or(kb == 0, all_dead)
        return kb - 1, done.astype(jnp.int32)

    lax.while_loop(lambda c: c[1] == 0, body, (qi, jnp.int32(0)))
    o_ref[0] = acc_ref[...].astype(o_ref.dtype)


def _stickbreak(q, k, v):
    b, s, w = q.shape
    r = _suffix_sum_matrix()
    blk = pl.BlockSpec((1, BLOCK, w), lambda bi, qi: (bi, qi, 0))
    full = pl.BlockSpec((1, s, w), lambda bi, qi: (bi, 0, 0))
    return pl.pallas_call(
        _stickbreak_kernel,
        grid=(b, s // BLOCK),
        in_specs=[blk, full, full, pl.BlockSpec(r.shape, lambda bi, qi: (0, 0))],
        out_specs=blk,
        out_shape=jax.ShapeDtypeStruct((b, s, w), BF16),
        scratch_shapes=[pltpu.VMEM((BLOCK, 2 * w), F32), pltpu.VMEM((BLOCK, w), F32)],
        compiler_params=_params("arbitrary", "arbitrary"),
        name="stickbreak",
    )(q, k, v, r)


MIX_ROWS = 512


def _mix_xattn_kernel(o0, o1, o2, l0, l1, l2, sb, ga, gb, x_ref, wa, wb, wo, gx, wxq, kv, wxo, out_ref):
    la, lb, lc = l0[...], l1[...], l2[...]
    m = jnp.maximum(jnp.maximum(la, lb), lc)
    ea, eb, ec = jnp.exp(la - m), jnp.exp(lb - m), jnp.exp(lc - m)
    y = (ea * o0[...].astype(F32) + eb * o1[...].astype(F32) + ec * o2[...].astype(F32)) * (1.0 / (ea + eb + ec))
    ya = _dot(y.astype(BF16), wa[...])
    yb = _dot(sb[...], wb[...])
    mix = ga[...].astype(F32) * ya + gb[...].astype(F32) * yb
    h = x_ref[...] + _dot(mix.astype(BF16), wo[...])

    n = _rms_norm_bf16(h, gx[...])
    q = _dot(n, wxq[...])
    scale = X_HEAD_DIM ** -0.5
    heads = []
    for hd in range(X_HEADS):
        cols = slice(hd * X_HEAD_DIM, (hd + 1) * X_HEAD_DIM)
        kh = kv[0, :, cols]
        vh = kv[0, :, X_WIDTH + hd * X_HEAD_DIM:X_WIDTH + (hd + 1) * X_HEAD_DIM]
        s = _dot_nt(q[:, cols].astype(BF16), kh) * scale
        mx = jnp.max(s, axis=1, keepdims=True)
        p = jnp.exp(s - mx)
        l = jnp.sum(p, axis=1, keepdims=True)
        heads.append((_dot(p.astype(BF16), vh) * (1.0 / l)).astype(BF16))
    o = jnp.concatenate(heads, axis=1)
    out_ref[...] = h + _dot(o, wxo[...])


def _mix_xattn(o_groups, lse_groups, sb, ga, gb, x2, wa, wb, wo, gx, wxq, kv, wxo, seq):
    t_rows, d = x2.shape
    tm = MIX_ROWS
    tiles_per_seq = seq // tm
    row = lambda width: pl.BlockSpec((tm, width), lambda i: (i, 0))
    const = lambda arr: pl.BlockSpec(arr.shape, lambda i: (0,) * arr.ndim)
    ml, kvw = kv.shape[1], kv.shape[2]
    return pl.pallas_call(
        _mix_xattn_kernel,
        grid=(t_rows // tm,),
        in_specs=[row(A_GROUP_WIDTH)] * 6 + [row(SB_WIDTH), row(d), row(d), row(d),
                  const(wa), const(wb), const(wo), const(gx), const(wxq),
                  pl.BlockSpec((1, ml, kvw), lambda i: (i // tiles_per_seq, 0, 0)),
                  const(wxo)],
        out_specs=row(d),
        out_shape=jax.ShapeDtypeStruct((t_rows, d), F32),
        compiler_params=_params("arbitrary"),
        name="mix_xattn",
    )(*o_groups, *lse_groups, sb, ga, gb, x2, wa, wb, wo, gx, wxq, kv, wxo)


FFN_ROWS = 256
FFN_CHUNK = 256
HALO = SUBLANES


def _ffn_kernel(h_ref, g_ref, wup, cw_ref, cb_ref, wdn, gf_ref, out_ref, halo_ref, u_ref, act_ref,
                *, tiles_per_seq):
    i = pl.program_id(0)
    rows = h_ref.shape[0]
    d_ff = wdn.shape[0]

    @pl.when(i % tiles_per_seq == 0)
    def _():
        halo_ref[...] = jnp.zeros_like(halo_ref)

    h = h_ref[...]
    n = _rms_norm_bf16(h, g_ref[...])

    def conv(c0):
        cols = slice(c0, c0 + FFN_CHUNK)
        u = _dot(n, wup[:, cols])
        u_ref[0:HALO, :] = halo_ref[:, cols]
        u_ref[HALO:HALO + rows, :] = u
        halo_ref[:, cols] = u[rows - HALO:, :]
        out = cb_ref[:, cols] + cw_ref[CONV_WIDTH - 1:CONV_WIDTH, cols] * u
        for back in range(1, CONV_WIDTH):
            tap = CONV_WIDTH - 1 - back
            out = out + cw_ref[tap:tap + 1, cols] * u_ref[HALO - back:HALO - back + rows, :]
        return out

    for c in range(d_ff // FFN_CHUNK):
        gate = conv(c * FFN_CHUNK)
        val = conv(d_ff + c * FFN_CHUNK)
        act = gate * (1.0 / (1.0 + jnp.exp(-gate))) * val
        act_ref[:, c * FFN_CHUNK:(c + 1) * FFN_CHUNK] = act.astype(BF16)

    y = h + _dot(act_ref[...], wdn[...])
    ms = jnp.mean(y * y, axis=-1, keepdims=True)
    out_ref[...] = y * lax.rsqrt(ms + EPS) * gf_ref[...]


def _ffn(h2, g, wup, conv_w, conv_b, wdn, gf, seq):
    t_rows, d = h2.shape
    tm = FFN_ROWS
    d_ff = wdn.shape[0]
    assert d_ff % FFN_CHUNK == 0 and seq % tm == 0
    const = lambda arr: pl.BlockSpec(arr.shape, lambda i: (0,) * arr.ndim)
    return pl.pallas_call(
        functools.partial(_ffn_kernel, tiles_per_seq=seq // tm),
        grid=(t_rows // tm,),
        in_specs=[pl.BlockSpec((tm, d), lambda i: (i, 0)),
                  const(g), const(wup), const(conv_w), const(conv_b), const(wdn), const(gf)],
        out_specs=pl.BlockSpec((tm, d), lambda i: (i, 0)),
        out_shape=jax.ShapeDtypeStruct((t_rows, d), F32),
        scratch_shapes=[pltpu.VMEM((HALO, 2 * d_ff), F32),
                        pltpu.VMEM((HALO + tm, FFN_CHUNK), F32),
                        pltpu.VMEM((tm, d_ff), BF16)],
        compiler_params=_params("arbitrary"),
        name="ffn",
    )(h2, g, wup, conv_w, conv_b, wdn, gf)


def _rope_tables(seq):
    half = HEAD_DIM // 2
    inv_freq = 1.0 / (ROPE_THETA ** (jnp.arange(half, dtype=F32) * (2.0 / HEAD_DIM)))
    ang = jnp.arange(seq).astype(F32)[:, None] * inv_freq[None, :]
    cos, sin = jnp.cos(ang), jnp.sin(ang)
    reps = LANES // HEAD_DIM
    cos_t = jnp.tile(jnp.concatenate([cos, cos], axis=1), (1, reps))
    sin_t = jnp.tile(jnp.concatenate([-sin, sin], axis=1), (1, reps))
    return cos_t, sin_t


def kernel(x, mem, ln_mix_g, w_in, b_gate, w_branch_a, w_branch_b, w_out, ln_x_g, ln_mem_g, w_xq, w_xkv, w_xo,
           ln_ffn_g, w_up, conv_w, conv_b, w_down, ln_f_g):
    b, s, d = x.shape
    depth = w_in.shape[0]
    assert depth == 1, "the final norm is fused into the (single) layer's ffn kernel"
    cos_t, sin_t = _rope_tables(s)
    h = x.reshape(b * s, d)
    for l in range(depth):
        (qa0, qa1, qa2, ka0, ka1, ka2, va0, va1, va2, qb, kb, vb, ga, gb) = _in_proj(
            h, ln_mix_g[l][None], w_in[l].astype(BF16), b_gate[l][None], cos_t, sin_t, s)
        o_groups, lse_groups = [], []
        for g, ((window, dilation), q, k, v) in enumerate(zip(
                DIL_GROUPS, (qa0, qa1, qa2), (ka0, ka1, ka2), (va0, va1, va2))):
            shape3 = (b, s, A_GROUP_WIDTH)
            o, lse = _dil_attn(q.reshape(shape3), k.reshape(shape3), v.reshape(shape3),
                               dilation, window // dilation, f"dil_attn_g{g}")
            o_groups.append(o)
            lse_groups.append(lse)
        sb3 = (b, s, SB_WIDTH)
        sb = _stickbreak(qb.reshape(sb3), kb.reshape(sb3), vb.reshape(sb3)).reshape(b * s, SB_WIDTH)
        kv = _mem_kv(mem, ln_mem_g[l][None], w_xkv[l].astype(BF16))
        h = _mix_xattn(o_groups, lse_groups, sb, ga, gb, h,
                       w_branch_a[l].astype(BF16), w_branch_b[l].astype(BF16), w_out[l].astype(BF16),
                       ln_x_g[l][None], w_xq[l].astype(BF16), kv, w_xo[l].astype(BF16), s)
        h = _ffn(h, ln_ffn_g[l][None], w_up[l].astype(BF16), conv_w[l], conv_b[l][None],
                 w_down[l].astype(BF16), ln_f_g[None], s)
    return h.reshape(b, s, d)
```

```python
import functools

import jax
import jax.numpy as jnp
from jax import lax
from jax.experimental import pallas as pl
from jax.experimental.pallas import tpu as pltpu

F32 = jnp.float32
BF16 = jnp.bfloat16

LANES = 128
SUBLANES = 8
VMEM_LIMIT_BYTES = 56 * 1024 * 1024

HEAD_DIM = 64
DIL_GROUPS = ((128, 1), (512, 4), (2048, 16))
A_HEADS_PER_GROUP = 4
A_GROUP_WIDTH = A_HEADS_PER_GROUP * HEAD_DIM
SB_HEADS = 8
SB_WIDTH = SB_HEADS * HEAD_DIM
BLOCK = 128
ROPE_THETA = 10000.0
X_HEADS = 4
X_HEAD_DIM = 128
X_WIDTH = X_HEADS * X_HEAD_DIM
CONV_WIDTH = 3
EPS = 1e-6

F32_EXP_ZERO_BELOW = -104.0
MASKED_SCORE = -1e30


def _rms_norm_bf16(x, g):
    ms = jnp.mean(x * x, axis=-1, keepdims=True)
    return (x * lax.rsqrt(ms + EPS) * g).astype(BF16)


def _dot(a, b):
    return jnp.dot(a, b, preferred_element_type=F32)


def _dot_nt(a, b):
    return lax.dot_general(a, b, (((1,), (1,)), ((), ())), preferred_element_type=F32)


def _params(*semantics):
    return pltpu.CompilerParams(dimension_semantics=semantics, vmem_limit_bytes=VMEM_LIMIT_BYTES)


def _mem_kv_kernel(m_ref, g_ref, w_ref, o_ref):
    n = _rms_norm_bf16(m_ref[0], g_ref[...])
    o_ref[0] = _dot(n, w_ref[...]).astype(BF16)


def _mem_kv(mem, g, w_bf16):
    b, ml, d = mem.shape
    n_out = w_bf16.shape[1]
    return pl.pallas_call(
        _mem_kv_kernel,
        grid=(b,),
        in_specs=[pl.BlockSpec((1, ml, d), lambda i: (i, 0, 0)),
                  pl.BlockSpec((1, d), lambda i: (0, 0)),
                  pl.BlockSpec((d, n_out), lambda i: (0, 0))],
        out_specs=pl.BlockSpec((1, ml, n_out), lambda i: (i, 0, 0)),
        out_shape=jax.ShapeDtypeStruct((b, ml, n_out), BF16),
        compiler_params=_params("arbitrary"),
        name="mem_kv",
    )(mem, g, w_bf16)


IN_PROJ_ROWS = 512


def _in_proj_kernel(x_ref, g_ref, w_ref, bg_ref, cos_ref, sin_ref,
                    qa0, qa1, qa2, ka0, ka1, ka2, va0, va1, va2, qb, kb, vb, ga, gb, stage_ref):
    n = _rms_norm_bf16(x_ref[...], g_ref[...])
    rows = n.shape[0]
    cos = cos_ref[...]
    sin = sin_ref[...]
    lane = lax.broadcasted_iota(jnp.int32, (rows, LANES), 1)
    first_half = (lane % HEAD_DIM) < (HEAD_DIM // 2)

    def rope(t):
        partner = jnp.where(first_half, pltpu.roll(t, LANES - HEAD_DIM // 2, 1),
                            pltpu.roll(t, HEAD_DIM // 2, 1))
        return t * cos + partner * sin

    def proj(c0, width):
        return _dot(n, w_ref[:, c0:c0 + width])

    def store_dilated(ref, t, dilation):
        if dilation == 1:
            ref[...] = t.astype(BF16)
            return
        n_planes = A_GROUP_WIDTH // LANES
        for c in range(n_planes):
            stage_ref[c] = t[:, c * LANES:(c + 1) * LANES]
        for r in range(dilation):
            for c in range(n_planes):
                c0 = r * A_GROUP_WIDTH + c * LANES
                ref[:, c0:c0 + LANES] = stage_ref[c, pl.ds(r, rows // dilation, stride=dilation), :].astype(BF16)

    q_scale = HEAD_DIM ** -0.5
    dilations = [d for _, d in DIL_GROUPS]
    col = 0
    for refs, roped, scale in (((qa0, qa1, qa2), True, q_scale), ((ka0, ka1, ka2), True, None),
                               ((va0, va1, va2), False, None)):
        for ref, dilation in zip(refs, dilations):
            t = proj(col, A_GROUP_WIDTH)
            if roped:
                t = jnp.concatenate([rope(t[:, h * LANES:(h + 1) * LANES])
                                     for h in range(A_GROUP_WIDTH // LANES)], axis=1)
            if scale is not None:
                t = t * scale
            store_dilated(ref, t, dilation)
            col += A_GROUP_WIDTH
    chunk = 256
    for ref, scale in ((qb, q_scale), (kb, None), (vb, None)):
        for c in range(SB_WIDTH // chunk):
            t = proj(col, chunk)
            if scale is not None:
                t = t * scale
            ref[:, c * chunk:(c + 1) * chunk] = t.astype(BF16)
            col += chunk
    d_model = ga.shape[1]
    gcol = 0
    for ref in (ga, gb):
        for c in range(d_model // chunk):
            t = proj(col, chunk) + bg_ref[:, gcol:gcol + chunk]
            ref[:, c * chunk:(c + 1) * chunk] = (1.0 / (1.0 + jnp.exp(-t))).astype(BF16)
            col += chunk
            gcol += chunk


def _in_proj(x2, g, w_bf16, b_gate, cos_t, sin_t, seq):
    t_rows, d = x2.shape
    tm = IN_PROJ_ROWS
    n_cols = w_bf16.shape[1]
    tiles_per_seq = seq // tm
    dilations = [dil for _, dil in DIL_GROUPS]
    outs = [(dil, dil * A_GROUP_WIDTH) for dil in dilations] * 3 + [(1, SB_WIDTH)] * 3 + [(1, d)] * 2
    return pl.pallas_call(
        _in_proj_kernel,
        grid=(t_rows // tm,),
        in_specs=[pl.BlockSpec((tm, d), lambda i: (i, 0)),
                  pl.BlockSpec((1, d), lambda i: (0, 0)),
                  pl.BlockSpec((d, n_cols), lambda i: (0, 0)),
                  pl.BlockSpec((1, 2 * d), lambda i: (0, 0)),
                  pl.BlockSpec((tm, LANES), lambda i: (i % tiles_per_seq, 0)),
                  pl.BlockSpec((tm, LANES), lambda i: (i % tiles_per_seq, 0))],
        out_specs=[pl.BlockSpec((tm // div, w), lambda i: (i, 0)) for div, w in outs],
        out_shape=[jax.ShapeDtypeStruct((t_rows // div, w), BF16) for div, w in outs],
        scratch_shapes=[pltpu.VMEM((A_GROUP_WIDTH // LANES, tm, LANES), F32)],
        compiler_params=_params("arbitrary"),
        name="in_proj",
    )(x2, g, w_bf16, b_gate, cos_t, sin_t)


def _dil_attn_kernel(q_ref, kc_ref, kp_ref, vc_ref, vp_ref, o_ref, lse_ref, *, n_back):
    j = pl.program_id(2)
    rows = q_ref.shape[1]
    row = lax.broadcasted_iota(jnp.int32, (BLOCK, 2 * BLOCK), 0)
    col = lax.broadcasted_iota(jnp.int32, (BLOCK, 2 * BLOCK), 1)
    dist = row - col + BLOCK
    band = (dist >= 0) & (dist <= n_back)
    first_col = jnp.where(j > 0, 0, BLOCK)
    band_first = band & (col >= first_col)
    lane = lax.broadcasted_iota(jnp.int32, (BLOCK, LANES), 1)
    head0 = lane < HEAD_DIM

    for sub in range(rows // BLOCK):
        cur = slice(sub * BLOCK, (sub + 1) * BLOCK)
        q = q_ref[0, cur, :]
        if sub == 0:
            k2 = jnp.concatenate([kp_ref[0], kc_ref[0, cur, :]], axis=0)
            v2 = jnp.concatenate([vp_ref[0], vc_ref[0, cur, :]], axis=0)
            mask = band_first
        else:
            both = slice((sub - 1) * BLOCK, (sub + 1) * BLOCK)
            k2 = kc_ref[0, both, :]
            v2 = vc_ref[0, both, :]
            mask = band
        outs, lses = [], []
        for is_head0 in (True, False):
            sel = head0 if is_head0 else jnp.logical_not(head0)
            qh = jnp.where(sel, q, jnp.zeros_like(q))
            s = jnp.where(mask, _dot_nt(qh, k2), MASKED_SCORE)
            m = jnp.max(s, axis=1, keepdims=True)
            p = jnp.exp(s - m)
            l = jnp.sum(p, axis=1, keepdims=True)
            outs.append(_dot(p.astype(BF16), v2) * (1.0 / l))
            lses.append(m + jnp.log(l))
        o_ref[0, cur, :] = jnp.where(head0, outs[0], outs[1]).astype(o_ref.dtype)
        lse_ref[0, cur, :] = jnp.where(head0, lses[0], lses[1])


def _dil_attn(q, k, v, batch, n_back, name):
    assert n_back <= BLOCK
    total_rows, width = q.shape
    length = total_rows // batch
    rows = min(length, 4 * BLOCK)
    blocks_per_chunk = rows // BLOCK
    view = lambda t: t.reshape(batch, length, width)
    cur_spec = pl.BlockSpec((1, rows, LANES), lambda bi, c, j: (bi, j, c))
    prev_spec = pl.BlockSpec((1, BLOCK, LANES),
                             lambda bi, c, j: (bi, jnp.maximum(j * blocks_per_chunk - 1, 0), c))
    o, lse = pl.pallas_call(
        functools.partial(_dil_attn_kernel, n_back=n_back),
        grid=(batch, width // LANES, length // rows),
        in_specs=[cur_spec, cur_spec, prev_spec, cur_spec, prev_spec],
        out_specs=[cur_spec, cur_spec],
        out_shape=[jax.ShapeDtypeStruct((batch, length, width), BF16),
                   jax.ShapeDtypeStruct((batch, length, width), F32)],
        compiler_params=_params("arbitrary", "arbitrary", "arbitrary"),
        name=name,
    )(view(q), view(k), view(k), view(v), view(v))
    return o.reshape(total_rows, width), lse.reshape(total_rows, width)


SB_FIRST_BLOCKS = 3
SB_GROUP_HEADS = 4
SB_GROUP_WIDTH = SB_GROUP_HEADS * HEAD_DIM


def _suffix_sum_matrix(n_keys):
    j = jnp.arange(n_keys)[:, None]
    s = jnp.arange(n_keys)[None, :]
    u = (j > s).astype(BF16)
    return jnp.concatenate([u, u], axis=0)


def _stickbreak_kernel(q_ref, k_ref, v_ref, u_ref, o_ref, c_ref, acc_ref):
    qi = pl.program_id(1)
    n_groups = q_ref.shape[2] // SB_GROUP_WIDTH
    c_ref[...] = jnp.zeros_like(c_ref)
    acc_ref[...] = jnp.zeros_like(acc_ref)

    def window(k0, n_blocks, mask_from):
        n_keys = n_blocks * BLOCK
        head_of_lane = lax.broadcasted_iota(jnp.int32, (n_keys, SB_GROUP_WIDTH), 1) // HEAD_DIM

        def split_heads(t):
            zero = jnp.zeros_like(t)
            return jnp.concatenate([jnp.where(head_of_lane == h, t, zero) for h in range(SB_GROUP_HEADS)], axis=0)

        q_pos = qi * BLOCK + lax.broadcasted_iota(jnp.int32, (BLOCK, BLOCK), 0)
        key_in_block = lax.broadcasted_iota(jnp.int32, (BLOCK, BLOCK), 1)
        valid = {j: (k0 + j * BLOCK + key_in_block) < q_pos for j in range(mask_from, n_blocks)}

        for g in range(n_groups):
            cols = slice(g * SB_GROUP_WIDTH, (g + 1) * SB_GROUP_WIDTH)
            q = q_ref[0, :, cols]
            kst = split_heads(k_ref[0, pl.ds(k0, n_keys), cols])
            vst = split_heads(v_ref[0, pl.ds(k0, n_keys), cols])
            z4 = _dot_nt(q, kst)
            log_betas, split, later, total = {}, [], {}, []
            for h in range(SB_GROUP_HEADS):
                run = None
                for j in reversed(range(n_blocks)):
                    c0 = h * n_keys + j * BLOCK
                    z = z4[:, c0:c0 + BLOCK]
                    soft = jnp.log1p(jnp.exp(-jnp.abs(z)))
                    log_keep = -jnp.maximum(z, 0.0) - soft
                    log_beta = jnp.minimum(z, 0.0) - soft
                    if j in valid:
                        log_keep = jnp.where(valid[j], log_keep, 0.0)
                        log_beta = jnp.where(valid[j], log_beta, MASKED_SCORE)
                    hi = log_keep.astype(BF16)
                    lo = (log_keep - hi.astype(F32)).astype(BF16)
                    log_betas[h, j] = log_beta
                    split.append(((h, j), jnp.concatenate([hi, lo], axis=1)))
                    later[h, j] = run
                    tot = jnp.sum(log_keep, axis=1, keepdims=True)
                    run = tot if run is None else run + tot
                total.append(run)
            split.sort(key=lambda item: item[0])
            within = _dot(jnp.concatenate([t for _, t in split], axis=0), u_ref[...])
            weights = []
            for h in range(SB_GROUP_HEADS):
                head = g * SB_GROUP_HEADS + h
                ccols = slice(head * LANES, (head + 1) * LANES)
                c_old = c_ref[:, ccols]
                for j in range(n_blocks):
                    r0 = (h * n_blocks + j) * BLOCK
                    base = c_old if later[h, j] is None else c_old + later[h, j]
                    after = base + within[r0:r0 + BLOCK]
                    weights.append(jnp.exp(log_betas[h, j] + after).astype(BF16))
                c_ref[:, ccols] = c_old + total[h]
            acc_ref[:, cols] += _dot(jnp.concatenate(weights, axis=1), vst)

    def all_dead():
        return jnp.max(c_ref[...]) < F32_EXP_ZERO_BELOW

    @pl.when(qi >= SB_FIRST_BLOCKS - 1)
    def _():
        first = qi - (SB_FIRST_BLOCKS - 1)
        window(pl.multiple_of(first * BLOCK, BLOCK), SB_FIRST_BLOCKS, SB_FIRST_BLOCKS - 1)

        def body(carry):
            kb, _ = carry
            window(pl.multiple_of(kb * BLOCK, BLOCK), 1, 1)
            done = jnp.logical_or(kb == 0, all_dead())
            return kb - 1, done.astype(jnp.int32)

        done_first = jnp.logical_or(first == 0, all_dead())
        lax.while_loop(lambda c: c[1] == 0, body, (first - 1, done_first.astype(jnp.int32)))

    @pl.when(qi < SB_FIRST_BLOCKS - 1)
    def _():
        window(0, SB_FIRST_BLOCKS - 1, 0)

    o_ref[0] = acc_ref[...].astype(o_ref.dtype)


def _stickbreak(q, k, v):
    b, s, w = q.shape
    assert s >= SB_FIRST_BLOCKS * BLOCK and w % SB_GROUP_WIDTH == 0
    u = _suffix_sum_matrix(BLOCK)
    blk = pl.BlockSpec((1, BLOCK, w), lambda bi, qi: (bi, qi, 0))
    full = pl.BlockSpec((1, s, w), lambda bi, qi: (bi, 0, 0))
    const = lambda arr: pl.BlockSpec(arr.shape, lambda bi, qi: (0, 0))
    n_heads = w // HEAD_DIM
    return pl.pallas_call(
        _stickbreak_kernel,
        grid=(b, s // BLOCK),
        in_specs=[blk, full, full, const(u)],
        out_specs=blk,
        out_shape=jax.ShapeDtypeStruct((b, s, w), BF16),
        scratch_shapes=[pltpu.VMEM((BLOCK, n_heads * LANES), F32), pltpu.VMEM((BLOCK, w), F32)],
        compiler_params=_params("arbitrary", "arbitrary"),
        name="stickbreak",
    )(q, k, v, u)


MIX_ROWS = 512


def _mix_xattn_kernel(o0, o1, o2, l0, l1, l2, sb, ga, gb, x_ref, wa, wb, wo, gx, wxq, kv, wxo, out_ref,
                      so1, sl1, so2, sl2):
    rows = x_ref.shape[0]

    def undilate(ref, stage_ref, dilation):
        n_planes = A_GROUP_WIDTH // LANES
        for r in range(dilation):
            for c in range(n_planes):
                c0 = r * A_GROUP_WIDTH + c * LANES
                stage_ref[c, pl.ds(r, rows // dilation, stride=dilation), :] = ref[:, c0:c0 + LANES].astype(F32)
        return jnp.concatenate([stage_ref[c] for c in range(n_planes)], axis=1)

    dilations = [d for _, d in DIL_GROUPS]
    assert dilations[0] == 1
    oa, la = o0[...].astype(F32), l0[...]
    ob, lb = undilate(o1, so1, dilations[1]), undilate(l1, sl1, dilations[1])
    oc, lc = undilate(o2, so2, dilations[2]), undilate(l2, sl2, dilations[2])
    m = jnp.maximum(jnp.maximum(la, lb), lc)
    ea, eb, ec = jnp.exp(la - m), jnp.exp(lb - m), jnp.exp(lc - m)
    y = (ea * oa + eb * ob + ec * oc) * (1.0 / (ea + eb + ec))
    ya = _dot(y.astype(BF16), wa[...])
    yb = _dot(sb[...], wb[...])
    mix = ga[...].astype(F32) * ya + gb[...].astype(F32) * yb
    h = x_ref[...] + _dot(mix.astype(BF16), wo[...])

    n = _rms_norm_bf16(h, gx[...])
    q = _dot(n, wxq[...])
    scale = X_HEAD_DIM ** -0.5
    heads = []
    for hd in range(X_HEADS):
        cols = slice(hd * X_HEAD_DIM, (hd + 1) * X_HEAD_DIM)
        kh = kv[0, :, cols]
        vh = kv[0, :, X_WIDTH + hd * X_HEAD_DIM:X_WIDTH + (hd + 1) * X_HEAD_DIM]
        s = _dot_nt(q[:, cols].astype(BF16), kh) * scale
        mx = jnp.max(s, axis=1, keepdims=True)
        p = jnp.exp(s - mx)
        l = jnp.sum(p, axis=1, keepdims=True)
        heads.append((_dot(p.astype(BF16), vh) * (1.0 / l)).astype(BF16))
    o = jnp.concatenate(heads, axis=1)
    out_ref[...] = h + _dot(o, wxo[...])


def _mix_xattn(o_groups, lse_groups, sb, ga, gb, x2, wa, wb, wo, gx, wxq, kv, wxo, seq):
    t_rows, d = x2.shape
    tm = MIX_ROWS
    tiles_per_seq = seq // tm
    row = lambda width: pl.BlockSpec((tm, width), lambda i: (i, 0))
    dil = lambda dilation: pl.BlockSpec((tm // dilation, dilation * A_GROUP_WIDTH), lambda i: (i, 0))
    const = lambda arr: pl.BlockSpec(arr.shape, lambda i: (0,) * arr.ndim)
    ml, kvw = kv.shape[1], kv.shape[2]
    group_specs = [dil(dilation) for _, dilation in DIL_GROUPS]
    return pl.pallas_call(
        _mix_xattn_kernel,
        grid=(t_rows // tm,),
        in_specs=group_specs + group_specs + [row(SB_WIDTH), row(d), row(d), row(d),
                  const(wa), const(wb), const(wo), const(gx), const(wxq),
                  pl.BlockSpec((1, ml, kvw), lambda i: (i // tiles_per_seq, 0, 0)),
                  const(wxo)],
        out_specs=row(d),
        out_shape=jax.ShapeDtypeStruct((t_rows, d), F32),
        scratch_shapes=[pltpu.VMEM((A_GROUP_WIDTH // LANES, tm, LANES), F32)] * 4,
        compiler_params=_params("arbitrary"),
        name="mix_xattn",
    )(*o_groups, *lse_groups, sb, ga, gb, x2, wa, wb, wo, gx, wxq, kv, wxo)


FFN_ROWS = 256
FFN_CHUNK = 256
HALO = SUBLANES


def _ffn_kernel(h_ref, g_ref, wup, cw_ref, cb_ref, wdn, gf_ref, out_ref, halo_ref, u_ref, act_ref,
                *, tiles_per_seq):
    i = pl.program_id(0)
    rows = h_ref.shape[0]
    d_ff = wdn.shape[0]

    @pl.when(i % tiles_per_seq == 0)
    def _():
        halo_ref[...] = jnp.zeros_like(halo_ref)

    h = h_ref[...]
    n = _rms_norm_bf16(h, g_ref[...])

    def conv(c0):
        cols = slice(c0, c0 + FFN_CHUNK)
        u = _dot(n, wup[:, cols])
        u_ref[0:HALO, :] = halo_ref[:, cols]
        u_ref[HALO:HALO + rows, :] = u
        halo_ref[:, cols] = u[rows - HALO:, :]
        out = cb_ref[:, cols] + cw_ref[CONV_WIDTH - 1:CONV_WIDTH, cols] * u
        for back in range(1, CONV_WIDTH):
            tap = CONV_WIDTH - 1 - back
            out = out + cw_ref[tap:tap + 1, cols] * u_ref[HALO - back:HALO - back + rows, :]
        return out

    for c in range(d_ff // FFN_CHUNK):
        gate = conv(c * FFN_CHUNK)
        val = conv(d_ff + c * FFN_CHUNK)
        act = gate * (1.0 / (1.0 + jnp.exp(-gate))) * val
        act_ref[:, c * FFN_CHUNK:(c + 1) * FFN_CHUNK] = act.astype(BF16)

    y = h + _dot(act_ref[...], wdn[...])
    ms = jnp.mean(y * y, axis=-1, keepdims=True)
    out_ref[...] = y * lax.rsqrt(ms + EPS) * gf_ref[...]


def _ffn(h2, g, wup, conv_w, conv_b, wdn, gf, seq):
    t_rows, d = h2.shape
    tm = FFN_ROWS
    d_ff = wdn.shape[0]
    assert d_ff % FFN_CHUNK == 0 and seq % tm == 0
    const = lambda arr: pl.BlockSpec(arr.shape, lambda i: (0,) * arr.ndim)
    return pl.pallas_call(
        functools.partial(_ffn_kernel, tiles_per_seq=seq // tm),
        grid=(t_rows // tm,),
        in_specs=[pl.BlockSpec((tm, d), lambda i: (i, 0)),
                  const(g), const(wup), const(conv_w), const(conv_b), const(wdn), const(gf)],
        out_specs=pl.BlockSpec((tm, d), lambda i: (i, 0)),
        out_shape=jax.ShapeDtypeStruct((t_rows, d), F32),
        scratch_shapes=[pltpu.VMEM((HALO, 2 * d_ff), F32),
                        pltpu.VMEM((HALO + tm, FFN_CHUNK), F32),
                        pltpu.VMEM((tm, d_ff), BF16)],
        compiler_params=_params("arbitrary"),
        name="ffn",
    )(h2, g, wup, conv_w, conv_b, wdn, gf)


def _rope_tables(seq):
    half = HEAD_DIM // 2
    inv_freq = 1.0 / (ROPE_THETA ** (jnp.arange(half, dtype=F32) * (2.0 / HEAD_DIM)))
    ang = jnp.arange(seq).astype(F32)[:, None] * inv_freq[None, :]
    cos, sin = jnp.cos(ang), jnp.sin(ang)
    reps = LANES // HEAD_DIM
    cos_t = jnp.tile(jnp.concatenate([cos, cos], axis=1), (1, reps))
    sin_t = jnp.tile(jnp.concatenate([-sin, sin], axis=1), (1, reps))
    return cos_t, sin_t


def kernel(x, mem, ln_mix_g, w_in, b_gate, w_branch_a, w_branch_b, w_out, ln_x_g, ln_mem_g, w_xq, w_xkv, w_xo,
           ln_ffn_g, w_up, conv_w, conv_b, w_down, ln_f_g):
    b, s, d = x.shape
    depth = w_in.shape[0]
    assert depth == 1, "the final norm is fused into the (single) layer's ffn kernel"
    cos_t, sin_t = _rope_tables(s)
    h = x.reshape(b * s, d)
    for l in range(depth):
        (qa0, qa1, qa2, ka0, ka1, ka2, va0, va1, va2, qb, kb, vb, ga, gb) = _in_proj(
            h, ln_mix_g[l][None], w_in[l].astype(BF16), b_gate[l][None], cos_t, sin_t, s)
        o_groups, lse_groups = [], []
        for g, ((window, dilation), q, k, v) in enumerate(zip(
                DIL_GROUPS, (qa0, qa1, qa2), (ka0, ka1, ka2), (va0, va1, va2))):
            o, lse = _dil_attn(q, k, v, b, window // dilation, f"dil_attn_g{g}")
            o_groups.append(o)
            lse_groups.append(lse)
        sb3 = (b, s, SB_WIDTH)
        sb = _stickbreak(qb.reshape(sb3), kb.reshape(sb3), vb.reshape(sb3)).reshape(b * s, SB_WIDTH)
        kv = _mem_kv(mem, ln_mem_g[l][None], w_xkv[l].astype(BF16))
        h = _mix_xattn(o_groups, lse_groups, sb, ga, gb, h,
                       w_branch_a[l].astype(BF16), w_branch_b[l].astype(BF16), w_out[l].astype(BF16),
                       ln_x_g[l][None], w_xq[l].astype(BF16), kv, w_xo[l].astype(BF16), s)
        h = _ffn(h, ln_ffn_g[l][None], w_up[l].astype(BF16), conv_w[l], conv_b[l][None],
                 w_down[l].astype(BF16), ln_f_g[None], s)
    return h.reshape(b, s, d)
```

```python
import functools

import jax
import jax.numpy as jnp
from jax import lax
from jax.experimental import pallas as pl
from jax.experimental.pallas import tpu as pltpu

F32 = jnp.float32
BF16 = jnp.bfloat16

LANES = 128
SUBLANES = 8
VMEM_LIMIT_BYTES = 56 * 1024 * 1024

HEAD_DIM = 64
DIL_GROUPS = ((128, 1), (512, 4), (2048, 16))
A_HEADS_PER_GROUP = 4
A_GROUP_WIDTH = A_HEADS_PER_GROUP * HEAD_DIM
SB_HEADS = 8
SB_WIDTH = SB_HEADS * HEAD_DIM
BLOCK = 128
ROPE_THETA = 10000.0
X_HEADS = 4
X_HEAD_DIM = 128
X_WIDTH = X_HEADS * X_HEAD_DIM
CONV_WIDTH = 3
EPS = 1e-6

F32_EXP_ZERO_BELOW = -104.0
MASKED_SCORE = -1e30


def _rms_norm_bf16(x, g):
    ms = jnp.mean(x * x, axis=-1, keepdims=True)
    return (x * lax.rsqrt(ms + EPS) * g).astype(BF16)


def _dot(a, b):
    return jnp.dot(a, b, preferred_element_type=F32)


def _dot_nt(a, b):
    return lax.dot_general(a, b, (((1,), (1,)), ((), ())), preferred_element_type=F32)


def _params(*semantics):
    return pltpu.CompilerParams(dimension_semantics=semantics, vmem_limit_bytes=VMEM_LIMIT_BYTES)


def _mem_kv_kernel(m_ref, g_ref, w_ref, o_ref):
    n = _rms_norm_bf16(m_ref[0], g_ref[...])
    o_ref[0] = _dot(n, w_ref[...]).astype(BF16)


def _mem_kv(mem, g, w_bf16):
    b, ml, d = mem.shape
    n_out = w_bf16.shape[1]
    return pl.pallas_call(
        _mem_kv_kernel,
        grid=(b,),
        in_specs=[pl.BlockSpec((1, ml, d), lambda i: (i, 0, 0)),
                  pl.BlockSpec((1, d), lambda i: (0, 0)),
                  pl.BlockSpec((d, n_out), lambda i: (0, 0))],
        out_specs=pl.BlockSpec((1, ml, n_out), lambda i: (i, 0, 0)),
        out_shape=jax.ShapeDtypeStruct((b, ml, n_out), BF16),
        compiler_params=_params("arbitrary"),
        name="mem_kv",
    )(mem, g, w_bf16)


IN_PROJ_ROWS = 512


def _in_proj_kernel(x_ref, g_ref, w_ref, bg_ref, cos_ref, sin_ref,
                    qa0, qa1, qa2, ka0, ka1, ka2, va0, va1, va2, qb, kb, vb, ga, gb, stage_ref):
    n = _rms_norm_bf16(x_ref[...], g_ref[...])
    rows = n.shape[0]
    cos = cos_ref[...]
    sin = sin_ref[...]
    lane = lax.broadcasted_iota(jnp.int32, (rows, LANES), 1)
    first_half = (lane % HEAD_DIM) < (HEAD_DIM // 2)

    def rope(t):
        partner = jnp.where(first_half, pltpu.roll(t, LANES - HEAD_DIM // 2, 1),
                            pltpu.roll(t, HEAD_DIM // 2, 1))
        return t * cos + partner * sin

    def proj(c0, width):
        return _dot(n, w_ref[:, c0:c0 + width])

    def store_dilated(ref, t, dilation):
        if dilation == 1:
            ref[...] = t.astype(BF16)
            return
        n_planes = A_GROUP_WIDTH // LANES
        for c in range(n_planes):
            stage_ref[c] = t[:, c * LANES:(c + 1) * LANES]
        for r in range(dilation):
            for c in range(n_planes):
                c0 = r * A_GROUP_WIDTH + c * LANES
                ref[:, c0:c0 + LANES] = stage_ref[c, pl.ds(r, rows // dilation, stride=dilation), :].astype(BF16)

    q_scale = HEAD_DIM ** -0.5
    dilations = [d for _, d in DIL_GROUPS]
    col = 0
    for refs, roped, scale in (((qa0, qa1, qa2), True, q_scale), ((ka0, ka1, ka2), True, None),
                               ((va0, va1, va2), False, None)):
        for ref, dilation in zip(refs, dilations):
            t = proj(col, A_GROUP_WIDTH)
            if roped:
                t = jnp.concatenate([rope(t[:, h * LANES:(h + 1) * LANES])
                                     for h in range(A_GROUP_WIDTH // LANES)], axis=1)
            if scale is not None:
                t = t * scale
            store_dilated(ref, t, dilation)
            col += A_GROUP_WIDTH
    chunk = 256
    for ref, scale in ((qb, q_scale), (kb, None), (vb, None)):
        for c in range(SB_WIDTH // chunk):
            t = proj(col, chunk)
            if scale is not None:
                t = t * scale
            ref[:, c * chunk:(c + 1) * chunk] = t.astype(BF16)
            col += chunk
    d_model = ga.shape[1]
    gcol = 0
    for ref in (ga, gb):
        for c in range(d_model // chunk):
            t = proj(col, chunk) + bg_ref[:, gcol:gcol + chunk]
            ref[:, c * chunk:(c + 1) * chunk] = (1.0 / (1.0 + jnp.exp(-t))).astype(BF16)
            col += chunk
            gcol += chunk


def _in_proj(x2, g, w_bf16, b_gate, cos_t, sin_t, seq):
    t_rows, d = x2.shape
    tm = IN_PROJ_ROWS
    n_cols = w_bf16.shape[1]
    tiles_per_seq = seq // tm
    dilations = [dil for _, dil in DIL_GROUPS]
    outs = [(dil, dil * A_GROUP_WIDTH) for dil in dilations] * 3 + [(1, SB_WIDTH)] * 3 + [(1, d)] * 2
    return pl.pallas_call(
        _in_proj_kernel,
        grid=(t_rows // tm,),
        in_specs=[pl.BlockSpec((tm, d), lambda i: (i, 0)),
                  pl.BlockSpec((1, d), lambda i: (0, 0)),
                  pl.BlockSpec((d, n_cols), lambda i: (0, 0)),
                  pl.BlockSpec((1, 2 * d), lambda i: (0, 0)),
                  pl.BlockSpec((tm, LANES), lambda i: (i % tiles_per_seq, 0)),
                  pl.BlockSpec((tm, LANES), lambda i: (i % tiles_per_seq, 0))],
        out_specs=[pl.BlockSpec((tm // div, w), lambda i: (i, 0)) for div, w in outs],
        out_shape=[jax.ShapeDtypeStruct((t_rows // div, w), BF16) for div, w in outs],
        scratch_shapes=[pltpu.VMEM((A_GROUP_WIDTH // LANES, tm, LANES), F32)],
        compiler_params=_params("arbitrary"),
        name="in_proj",
    )(x2, g, w_bf16, b_gate, cos_t, sin_t)


def _dil_attn_kernel(q_ref, kc_ref, kp_ref, vc_ref, vp_ref, o_ref, lse_ref, *, n_back):
    j = pl.program_id(2)
    rows, width = q_ref.shape[1], q_ref.shape[2]
    row = lax.broadcasted_iota(jnp.int32, (2 * BLOCK, 2 * BLOCK), 0) % BLOCK
    col = lax.broadcasted_iota(jnp.int32, (2 * BLOCK, 2 * BLOCK), 1)
    dist = row - col + BLOCK
    band = (dist >= 0) & (dist <= n_back)
    first_col = jnp.where(j > 0, 0, BLOCK)
    band_first = band & (col >= first_col)
    lane = lax.broadcasted_iota(jnp.int32, (BLOCK, LANES), 1)
    head0 = lane < HEAD_DIM

    for cb in range(width // LANES):
        lanes = slice(cb * LANES, (cb + 1) * LANES)
        for sub in range(rows // BLOCK):
            cur = slice(sub * BLOCK, (sub + 1) * BLOCK)
            q = q_ref[0, cur, lanes]
            if sub == 0:
                k2 = jnp.concatenate([kp_ref[0, :, lanes], kc_ref[0, cur, lanes]], axis=0)
                v2 = jnp.concatenate([vp_ref[0, :, lanes], vc_ref[0, cur, lanes]], axis=0)
                mask = band_first
            else:
                both = slice((sub - 1) * BLOCK, (sub + 1) * BLOCK)
                k2 = kc_ref[0, both, lanes]
                v2 = vc_ref[0, both, lanes]
                mask = band
            zero = jnp.zeros_like(q)
            q2 = jnp.concatenate([jnp.where(head0, q, zero), jnp.where(head0, zero, q)], axis=0)
            s = jnp.where(mask, _dot_nt(q2, k2), MASKED_SCORE)
            m = jnp.max(s, axis=1, keepdims=True)
            p = jnp.exp(s - m)
            l = jnp.sum(p, axis=1, keepdims=True)
            o2 = _dot(p.astype(BF16), v2) * (1.0 / l)
            lse2 = m + jnp.log(l)
            o_ref[0, cur, lanes] = jnp.where(head0, o2[:BLOCK], o2[BLOCK:]).astype(o_ref.dtype)
            lse_ref[0, cur, lanes] = jnp.where(head0, lse2[:BLOCK], lse2[BLOCK:])


DIL_ATTN_BLOCKS_PER_STEP = 8


def _dil_attn(q, k, v, batch, n_back, name):
    assert n_back <= BLOCK
    total_rows, width = q.shape
    length = total_rows // batch
    rows = min(length, 4 * BLOCK)
    blocks_per_chunk = rows // BLOCK
    chunk_width = min(width, LANES * DIL_ATTN_BLOCKS_PER_STEP // blocks_per_chunk)
    view = lambda t: t.reshape(batch, length, width)
    cur_spec = pl.BlockSpec((1, rows, chunk_width), lambda bi, c, j: (bi, j, c))
    prev_spec = pl.BlockSpec((1, BLOCK, chunk_width),
                             lambda bi, c, j: (bi, jnp.maximum(j * blocks_per_chunk - 1, 0), c))
    o, lse = pl.pallas_call(
        functools.partial(_dil_attn_kernel, n_back=n_back),
        grid=(batch, width // chunk_width, length // rows),
        in_specs=[cur_spec, cur_spec, prev_spec, cur_spec, prev_spec],
        out_specs=[cur_spec, cur_spec],
        out_shape=[jax.ShapeDtypeStruct((batch, length, width), BF16),
                   jax.ShapeDtypeStruct((batch, length, width), F32)],
        compiler_params=_params("arbitrary", "arbitrary", "arbitrary"),
        name=name,
    )(view(q), view(k), view(k), view(v), view(v))
    return o.reshape(total_rows, width), lse.reshape(total_rows, width)


SB_FIRST_BLOCKS = 3
SB_GROUP_HEADS = 4
SB_GROUP_WIDTH = SB_GROUP_HEADS * HEAD_DIM


def _suffix_sum_matrix(n_keys):
    j = jnp.arange(n_keys)[:, None]
    s = jnp.arange(n_keys)[None, :]
    u = (j > s).astype(BF16)
    return jnp.concatenate([u, u], axis=0)


def _stickbreak_kernel(q_ref, k_ref, v_ref, u_ref, o_ref, c_ref, acc_ref):
    qi = pl.program_id(1)
    n_groups = q_ref.shape[2] // SB_GROUP_WIDTH
    group_rows = SB_GROUP_HEADS * BLOCK
    group_cols = [slice(g * SB_GROUP_WIDTH, (g + 1) * SB_GROUP_WIDTH) for g in range(n_groups)]
    c_ref[...] = jnp.zeros_like(c_ref)
    acc_ref[...] = jnp.zeros_like(acc_ref)

    head_of_lane = lax.broadcasted_iota(jnp.int32, (BLOCK, SB_GROUP_WIDTH), 1) // HEAD_DIM

    def stack_heads(q):
        zero = jnp.zeros_like(q)
        return jnp.concatenate([jnp.where(head_of_lane == h, q, zero) for h in range(SB_GROUP_HEADS)], axis=0)

    q_stacked = [stack_heads(q_ref[0, :, cols]) for cols in group_cols]
    q_pos = qi * BLOCK + lax.broadcasted_iota(jnp.int32, (group_rows, BLOCK), 0) % BLOCK
    key_in_block = lax.broadcasted_iota(jnp.int32, (group_rows, BLOCK), 1)

    def window(k0, n_blocks, mask_from):
        n_keys = n_blocks * BLOCK
        valid = {j: (k0 + j * BLOCK + key_in_block) < q_pos for j in range(mask_from, n_blocks)}

        scores = [_dot_nt(q_stacked[g], k_ref[0, pl.ds(k0, n_keys), group_cols[g]])
                  for g in range(n_groups)]
        stage_b = []
        for g in range(n_groups):
            log_betas, split, later = {}, {}, {}
            run = None
            for j in reversed(range(n_blocks)):
                z = scores[g][:, j * BLOCK:(j + 1) * BLOCK]
                cost = jnp.maximum(z, 0.0) + jnp.log(1.0 + jnp.exp(-jnp.abs(z)))
                log_beta = z - cost
                if j in valid:
                    cost = jnp.where(valid[j], cost, 0.0)
                    log_beta = jnp.where(valid[j], log_beta, MASKED_SCORE)
                hi = cost.astype(BF16)
                lo = (cost - hi.astype(F32)).astype(BF16)
                log_betas[j] = log_beta
                split[j] = jnp.concatenate([hi, lo], axis=1)
                later[j] = run
                tot = jnp.sum(cost, axis=1, keepdims=True)
                run = tot if run is None else run + tot
            within = _dot(jnp.concatenate([split[j] for j in range(n_blocks)], axis=0), u_ref[...])
            stage_b.append((log_betas, later, run, within))
        for g in range(n_groups):
            log_betas, later, total, within = stage_b[g]
            rows = slice(g * group_rows, (g + 1) * group_rows)
            c_old = c_ref[rows, :]
            weights = []
            for j in range(n_blocks):
                base = c_old if later[j] is None else c_old + later[j]
                cost_after = base + within[j * group_rows:(j + 1) * group_rows]
                weights.append(jnp.exp(log_betas[j] - cost_after).astype(BF16))
            c_ref[rows, :] = c_old + total
            pv = _dot(jnp.concatenate(weights, axis=1), v_ref[0, pl.ds(k0, n_keys), group_cols[g]])
            picked = pv[(SB_GROUP_HEADS - 1) * BLOCK:]
            for h in reversed(range(SB_GROUP_HEADS - 1)):
                picked = jnp.where(head_of_lane == h, pv[h * BLOCK:(h + 1) * BLOCK], picked)
            acc_ref[:, group_cols[g]] += picked

    def all_dead():
        return jnp.min(c_ref[...]) > -F32_EXP_ZERO_BELOW

    @pl.when(qi >= SB_FIRST_BLOCKS - 1)
    def _():
        first = qi - (SB_FIRST_BLOCKS - 1)
        window(pl.multiple_of(first * BLOCK, BLOCK), SB_FIRST_BLOCKS, SB_FIRST_BLOCKS - 1)

        def body(carry):
            kb, _ = carry
            window(pl.multiple_of(kb * BLOCK, BLOCK), 1, 1)
            done = jnp.logical_or(kb == 0, all_dead())
            return kb - 1, done.astype(jnp.int32)

        done_first = jnp.logical_or(first == 0, all_dead())
        lax.while_loop(lambda c: c[1] == 0, body, (first - 1, done_first.astype(jnp.int32)))

    @pl.when(qi < SB_FIRST_BLOCKS - 1)
    def _():
        window(0, SB_FIRST_BLOCKS - 1, 0)

    o_ref[0] = acc_ref[...].astype(o_ref.dtype)


def _stickbreak(q, k, v):
    b, s, w = q.shape
    assert s >= SB_FIRST_BLOCKS * BLOCK and w % SB_GROUP_WIDTH == 0
    u = _suffix_sum_matrix(BLOCK)
    blk = pl.BlockSpec((1, BLOCK, w), lambda bi, qi: (bi, qi, 0))
    full = pl.BlockSpec((1, s, w), lambda bi, qi: (bi, 0, 0))
    const = lambda arr: pl.BlockSpec(arr.shape, lambda bi, qi: (0, 0))
    n_heads = w // HEAD_DIM
    return pl.pallas_call(
        _stickbreak_kernel,
        grid=(b, s // BLOCK),
        in_specs=[blk, full, full, const(u)],
        out_specs=blk,
        out_shape=jax.ShapeDtypeStruct((b, s, w), BF16),
        scratch_shapes=[pltpu.VMEM((n_heads * BLOCK, LANES), F32), pltpu.VMEM((BLOCK, w), F32)],
        compiler_params=_params("arbitrary", "arbitrary"),
        name="stickbreak",
    )(q, k, v, u)


MIX_ROWS = 512


def _mix_xattn_kernel(o0, o1, o2, l0, l1, l2, sb, ga, gb, x_ref, wa, wb, wo, gx, wxq, kv, wxo, out_ref,
                      so1, sl1, so2, sl2):
    rows = x_ref.shape[0]

    def undilate(ref, stage_ref, dilation):
        n_planes = A_GROUP_WIDTH // LANES
        for r in range(dilation):
            for c in range(n_planes):
                c0 = r * A_GROUP_WIDTH + c * LANES
                stage_ref[c, pl.ds(r, rows // dilation, stride=dilation), :] = ref[:, c0:c0 + LANES].astype(F32)
        return jnp.concatenate([stage_ref[c] for c in range(n_planes)], axis=1)

    dilations = [d for _, d in DIL_GROUPS]
    assert dilations[0] == 1
    oa, la = o0[...].astype(F32), l0[...]
    ob, lb = undilate(o1, so1, dilations[1]), undilate(l1, sl1, dilations[1])
    oc, lc = undilate(o2, so2, dilations[2]), undilate(l2, sl2, dilations[2])
    m = jnp.maximum(jnp.maximum(la, lb), lc)
    ea, eb, ec = jnp.exp(la - m), jnp.exp(lb - m), jnp.exp(lc - m)
    y = (ea * oa + eb * ob + ec * oc) * (1.0 / (ea + eb + ec))
    ya = _dot(y.astype(BF16), wa[...])
    yb = _dot(sb[...], wb[...])
    mix = ga[...].astype(F32) * ya + gb[...].astype(F32) * yb
    h = x_ref[...] + _dot(mix.astype(BF16), wo[...])

    n = _rms_norm_bf16(h, gx[...])
    q = _dot(n, wxq[...])
    scale = X_HEAD_DIM ** -0.5
    heads = []
    for hd in range(X_HEADS):
        cols = slice(hd * X_HEAD_DIM, (hd + 1) * X_HEAD_DIM)
        kh = kv[0, :, cols]
        vh = kv[0, :, X_WIDTH + hd * X_HEAD_DIM:X_WIDTH + (hd + 1) * X_HEAD_DIM]
        s = _dot_nt(q[:, cols].astype(BF16), kh) * scale
        mx = jnp.max(s, axis=1, keepdims=True)
        p = jnp.exp(s - mx)
        l = jnp.sum(p, axis=1, keepdims=True)
        heads.append((_dot(p.astype(BF16), vh) * (1.0 / l)).astype(BF16))
    o = jnp.concatenate(heads, axis=1)
    out_ref[...] = h + _dot(o, wxo[...])


def _mix_xattn(o_groups, lse_groups, sb, ga, gb, x2, wa, wb, wo, gx, wxq, kv, wxo, seq):
    t_rows, d = x2.shape
    tm = MIX_ROWS
    tiles_per_seq = seq // tm
    row = lambda width: pl.BlockSpec((tm, width), lambda i: (i, 0))
    dil = lambda dilation: pl.BlockSpec((tm // dilation, dilation * A_GROUP_WIDTH), lambda i: (i, 0))
    const = lambda arr: pl.BlockSpec(arr.shape, lambda i: (0,) * arr.ndim)
    ml, kvw = kv.shape[1], kv.shape[2]
    group_specs = [dil(dilation) for _, dilation in DIL_GROUPS]
    return pl.pallas_call(
        _mix_xattn_kernel,
        grid=(t_rows // tm,),
        in_specs=group_specs + group_specs + [row(SB_WIDTH), row(d), row(d), row(d),
                  const(wa), const(wb), const(wo), const(gx), const(wxq),
                  pl.BlockSpec((1, ml, kvw), lambda i: (i // tiles_per_seq, 0, 0)),
                  const(wxo)],
        out_specs=row(d),
        out_shape=jax.ShapeDtypeStruct((t_rows, d), F32),
        scratch_shapes=[pltpu.VMEM((A_GROUP_WIDTH // LANES, tm, LANES), F32)] * 4,
        compiler_params=_params("arbitrary"),
        name="mix_xattn",
    )(*o_groups, *lse_groups, sb, ga, gb, x2, wa, wb, wo, gx, wxq, kv, wxo)


FFN_ROWS = 256
FFN_CHUNK = 256
HALO = SUBLANES


def _ffn_kernel(h_ref, g_ref, wup, cw_ref, cb_ref, wdn, gf_ref, out_ref, halo_ref, u_ref, act_ref,
                *, tiles_per_seq):
    i = pl.program_id(0)
    rows = h_ref.shape[0]
    d_ff = wdn.shape[0]

    @pl.when(i % tiles_per_seq == 0)
    def _():
        halo_ref[...] = jnp.zeros_like(halo_ref)

    h = h_ref[...]
    n = _rms_norm_bf16(h, g_ref[...])

    def conv(c0):
        cols = slice(c0, c0 + FFN_CHUNK)
        u = _dot(n, wup[:, cols])
        u_ref[0:HALO, :] = halo_ref[:, cols]
        u_ref[HALO:HALO + rows, :] = u
        halo_ref[:, cols] = u[rows - HALO:, :]
        out = cb_ref[:, cols] + cw_ref[CONV_WIDTH - 1:CONV_WIDTH, cols] * u
        for back in range(1, CONV_WIDTH):
            tap = CONV_WIDTH - 1 - back
            out = out + cw_ref[tap:tap + 1, cols] * u_ref[HALO - back:HALO - back + rows, :]
        return out

    for c in range(d_ff // FFN_CHUNK):
        gate = conv(c * FFN_CHUNK)
        val = conv(d_ff + c * FFN_CHUNK)
        act = gate * (1.0 / (1.0 + jnp.exp(-gate))) * val
        act_ref[:, c * FFN_CHUNK:(c + 1) * FFN_CHUNK] = act.astype(BF16)

    y = h + _dot(act_ref[...], wdn[...])
    ms = jnp.mean(y * y, axis=-1, keepdims=True)
    out_ref[...] = y * lax.rsqrt(ms + EPS) * gf_ref[...]


def _ffn(h2, g, wup, conv_w, conv_b, wdn, gf, seq):
    t_rows, d = h2.shape
    tm = FFN_ROWS
    d_ff = wdn.shape[0]
    assert d_ff % FFN_CHUNK == 0 and seq % tm == 0
    const = lambda arr: pl.BlockSpec(arr.shape, lambda i: (0,) * arr.ndim)
    return pl.pallas_call(
        functools.partial(_ffn_kernel, tiles_per_seq=seq // tm),
        grid=(t_rows // tm,),
        in_specs=[pl.BlockSpec((tm, d), lambda i: (i, 0)),
                  const(g), const(wup), const(conv_w), const(conv_b), const(wdn), const(gf)],
        out_specs=pl.BlockSpec((tm, d), lambda i: (i, 0)),
        out_shape=jax.ShapeDtypeStruct((t_rows, d), F32),
        scratch_shapes=[pltpu.VMEM((HALO, 2 * d_ff), F32),
                        pltpu.VMEM((HALO + tm, FFN_CHUNK), F32),
                        pltpu.VMEM((tm, d_ff), BF16)],
        compiler_params=_params("arbitrary"),
        name="ffn",
    )(h2, g, wup, conv_w, conv_b, wdn, gf)


def _rope_tables(seq):
    half = HEAD_DIM // 2
    inv_freq = 1.0 / (ROPE_THETA ** (jnp.arange(half, dtype=F32) * (2.0 / HEAD_DIM)))
    ang = jnp.arange(seq).astype(F32)[:, None] * inv_freq[None, :]
    cos, sin = jnp.cos(ang), jnp.sin(ang)
    reps = LANES // HEAD_DIM
    cos_t = jnp.tile(jnp.concatenate([cos, cos], axis=1), (1, reps))
    sin_t = jnp.tile(jnp.concatenate([-sin, sin], axis=1), (1, reps))
    return cos_t, sin_t


def kernel(x, mem, ln_mix_g, w_in, b_gate, w_branch_a, w_branch_b, w_out, ln_x_g, ln_mem_g, w_xq, w_xkv, w_xo,
           ln_ffn_g, w_up, conv_w, conv_b, w_down, ln_f_g):
    b, s, d = x.shape
    depth = w_in.shape[0]
    assert depth == 1, "the final norm is fused into the (single) layer's ffn kernel"
    cos_t, sin_t = _rope_tables(s)
    h = x.reshape(b * s, d)
    for l in range(depth):
        (qa0, qa1, qa2, ka0, ka1, ka2, va0, va1, va2, qb, kb, vb, ga, gb) = _in_proj(
            h, ln_mix_g[l][None], w_in[l].astype(BF16), b_gate[l][None], cos_t, sin_t, s)
        o_groups, lse_groups = [], []
        for g, ((window, dilation), q, k, v) in enumerate(zip(
                DIL_GROUPS, (qa0, qa1, qa2), (ka0, ka1, ka2), (va0, va1, va2))):
            o, lse = _dil_attn(q, k, v, b, window // dilation, f"dil_attn_g{g}")
            o_groups.append(o)
            lse_groups.append(lse)
        sb3 = (b, s, SB_WIDTH)
        sb = _stickbreak(qb.reshape(sb3), kb.reshape(sb3), vb.reshape(sb3)).reshape(b * s, SB_WIDTH)
        kv = _mem_kv(mem, ln_mem_g[l][None], w_xkv[l].astype(BF16))
        h = _mix_xattn(o_groups, lse_groups, sb, ga, gb, h,
                       w_branch_a[l].astype(BF16), w_branch_b[l].astype(BF16), w_out[l].astype(BF16),
                       ln_x_g[l][None], w_xq[l].astype(BF16), kv, w_xo[l].astype(BF16), s)
        h = _ffn(h, ln_ffn_g[l][None], w_up[l].astype(BF16), conv_w[l], conv_b[l][None],
                 w_down[l].astype(BF16), ln_f_g[None], s)
    return h.reshape(b, s, d)
```

```python
import functools

import jax
import jax.numpy as jnp
from jax import lax
from jax.experimental import pallas as pl
from jax.experimental.pallas import tpu as pltpu

F32 = jnp.float32
BF16 = jnp.bfloat16

LANES = 128
SUBLANES = 8
VMEM_LIMIT_BYTES = 56 * 1024 * 1024

HEAD_DIM = 64
DIL_GROUPS = ((128, 1), (512, 4), (2048, 16))
A_HEADS_PER_GROUP = 4
A_GROUP_WIDTH = A_HEADS_PER_GROUP * HEAD_DIM
SB_HEADS = 8
SB_WIDTH = SB_HEADS * HEAD_DIM
BLOCK = 128
ROPE_THETA = 10000.0
X_HEADS = 4
X_HEAD_DIM = 128
X_WIDTH = X_HEADS * X_HEAD_DIM
CONV_WIDTH = 3
EPS = 1e-6

F32_EXP_ZERO_BELOW = -104.0
MASKED_SCORE = -1e30


def _rms_norm_bf16(x, g):
    ms = jnp.mean(x * x, axis=-1, keepdims=True)
    return (x * lax.rsqrt(ms + EPS) * g).astype(BF16)


def _dot(a, b):
    return jnp.dot(a, b, preferred_element_type=F32)


def _dot_nt(a, b):
    return lax.dot_general(a, b, (((1,), (1,)), ((), ())), preferred_element_type=F32)


def _params(*semantics):
    return pltpu.CompilerParams(dimension_semantics=semantics, vmem_limit_bytes=VMEM_LIMIT_BYTES)


def _mem_kv_kernel(m_ref, g_ref, w_ref, o_ref):
    n = _rms_norm_bf16(m_ref[0], g_ref[...])
    o_ref[0] = _dot(n, w_ref[...]).astype(BF16)


def _mem_kv(mem, g, w_bf16):
    b, ml, d = mem.shape
    n_out = w_bf16.shape[1]
    return pl.pallas_call(
        _mem_kv_kernel,
        grid=(b,),
        in_specs=[pl.BlockSpec((1, ml, d), lambda i: (i, 0, 0)),
                  pl.BlockSpec((1, d), lambda i: (0, 0)),
                  pl.BlockSpec((d, n_out), lambda i: (0, 0))],
        out_specs=pl.BlockSpec((1, ml, n_out), lambda i: (i, 0, 0)),
        out_shape=jax.ShapeDtypeStruct((b, ml, n_out), BF16),
        compiler_params=_params("arbitrary"),
        name="mem_kv",
    )(mem, g, w_bf16)


IN_PROJ_ROWS = 1024


def _in_proj_kernel(x_ref, g_ref, w_ref, bg_ref, cos_ref, sin_ref,
                    qa0, qa1, qa2, ka0, ka1, ka2, va0, va1, va2, qb, kb, vb, ga, gb, stage_ref):
    n = _rms_norm_bf16(x_ref[...], g_ref[...])
    rows = n.shape[0]
    cos = cos_ref[...]
    sin = sin_ref[...]
    lane = lax.broadcasted_iota(jnp.int32, (rows, LANES), 1)
    first_half = (lane % HEAD_DIM) < (HEAD_DIM // 2)

    def rope(t):
        partner = jnp.where(first_half, pltpu.roll(t, LANES - HEAD_DIM // 2, 1),
                            pltpu.roll(t, HEAD_DIM // 2, 1))
        return t * cos + partner * sin

    def proj(c0, width):
        return _dot(n, w_ref[:, c0:c0 + width])

    def store_dilated(ref, t, dilation):
        if dilation == 1:
            ref[...] = t.astype(BF16)
            return
        n_planes = A_GROUP_WIDTH // LANES
        for c in range(n_planes):
            stage_ref[c] = t[:, c * LANES:(c + 1) * LANES]
        for r in range(dilation):
            for c in range(n_planes):
                c0 = r * A_GROUP_WIDTH + c * LANES
                ref[:, c0:c0 + LANES] = stage_ref[c, pl.ds(r, rows // dilation, stride=dilation), :].astype(BF16)

    q_scale = HEAD_DIM ** -0.5
    dilations = [d for _, d in DIL_GROUPS]
    col = 0
    for refs, roped, scale in (((qa0, qa1, qa2), True, q_scale), ((ka0, ka1, ka2), True, None),
                               ((va0, va1, va2), False, None)):
        for ref, dilation in zip(refs, dilations):
            t = proj(col, A_GROUP_WIDTH)
            if roped:
                t = jnp.concatenate([rope(t[:, h * LANES:(h + 1) * LANES])
                                     for h in range(A_GROUP_WIDTH // LANES)], axis=1)
            if scale is not None:
                t = t * scale
            store_dilated(ref, t, dilation)
            col += A_GROUP_WIDTH
    chunk = 256
    for ref, scale in ((qb, q_scale), (kb, None), (vb, None)):
        for c in range(SB_WIDTH // chunk):
            t = proj(col, chunk)
            if scale is not None:
                t = t * scale
            ref[:, c * chunk:(c + 1) * chunk] = t.astype(BF16)
            col += chunk
    d_model = ga.shape[1]
    gcol = 0
    for ref in (ga, gb):
        for c in range(d_model // chunk):
            t = proj(col, chunk) + bg_ref[:, gcol:gcol + chunk]
            ref[:, c * chunk:(c + 1) * chunk] = (1.0 / (1.0 + jnp.exp(-t))).astype(BF16)
            col += chunk
            gcol += chunk


def _in_proj(x2, g, w_bf16, b_gate, cos_t, sin_t, seq):
    t_rows, d = x2.shape
    tm = IN_PROJ_ROWS
    n_cols = w_bf16.shape[1]
    tiles_per_seq = seq // tm
    dilations = [dil for _, dil in DIL_GROUPS]
    outs = [(dil, dil * A_GROUP_WIDTH) for dil in dilations] * 3 + [(1, SB_WIDTH)] * 3 + [(1, d)] * 2
    return pl.pallas_call(
        _in_proj_kernel,
        grid=(t_rows // tm,),
        in_specs=[pl.BlockSpec((tm, d), lambda i: (i, 0)),
                  pl.BlockSpec((1, d), lambda i: (0, 0)),
                  pl.BlockSpec((d, n_cols), lambda i: (0, 0), pipeline_mode=pl.Buffered(1)),
                  pl.BlockSpec((1, 2 * d), lambda i: (0, 0)),
                  pl.BlockSpec((tm, LANES), lambda i: (i % tiles_per_seq, 0)),
                  pl.BlockSpec((tm, LANES), lambda i: (i % tiles_per_seq, 0))],
        out_specs=[pl.BlockSpec((tm // div, w), lambda i: (i, 0)) for div, w in outs],
        out_shape=[jax.ShapeDtypeStruct((t_rows // div, w), BF16) for div, w in outs],
        scratch_shapes=[pltpu.VMEM((A_GROUP_WIDTH // LANES, tm, LANES), F32)],
        compiler_params=_params("arbitrary"),
        name="in_proj",
    )(x2, g, w_bf16, b_gate, cos_t, sin_t)


def _dil_attn_kernel(q_ref, kc_ref, kp_ref, vc_ref, vp_ref, o_ref, lse_ref, *, n_back):
    j = pl.program_id(2)
    rows, width = q_ref.shape[1], q_ref.shape[2]
    row = lax.broadcasted_iota(jnp.int32, (2 * BLOCK, 2 * BLOCK), 0) % BLOCK
    col = lax.broadcasted_iota(jnp.int32, (2 * BLOCK, 2 * BLOCK), 1)
    dist = row - col + BLOCK
    band = (dist >= 0) & (dist <= n_back)
    first_col = jnp.where(j > 0, 0, BLOCK)
    band_first = band & (col >= first_col)
    lane = lax.broadcasted_iota(jnp.int32, (BLOCK, LANES), 1)
    head0 = lane < HEAD_DIM

    for cb in range(width // LANES):
        lanes = slice(cb * LANES, (cb + 1) * LANES)
        for sub in range(rows // BLOCK):
            cur = slice(sub * BLOCK, (sub + 1) * BLOCK)
            q = q_ref[0, cur, lanes]
            if sub == 0:
                k2 = jnp.concatenate([kp_ref[0, :, lanes], kc_ref[0, cur, lanes]], axis=0)
                v2 = jnp.concatenate([vp_ref[0, :, lanes], vc_ref[0, cur, lanes]], axis=0)
                mask = band_first
            else:
                both = slice((sub - 1) * BLOCK, (sub + 1) * BLOCK)
                k2 = kc_ref[0, both, lanes]
                v2 = vc_ref[0, both, lanes]
                mask = band
            zero = jnp.zeros_like(q)
            q2 = jnp.concatenate([jnp.where(head0, q, zero), jnp.where(head0, zero, q)], axis=0)
            s = jnp.where(mask, _dot_nt(q2, k2), MASKED_SCORE)
            m = jnp.max(s, axis=1, keepdims=True)
            p = jnp.exp(s - m)
            l = jnp.sum(p, axis=1, keepdims=True)
            o2 = _dot(p.astype(BF16), v2) * (1.0 / l)
            lse2 = m + jnp.log(l)
            o_ref[0, cur, lanes] = jnp.where(head0, o2[:BLOCK], o2[BLOCK:]).astype(o_ref.dtype)
            lse_ref[0, cur, lanes] = jnp.where(head0, lse2[:BLOCK], lse2[BLOCK:])


DIL_ATTN_BLOCKS_PER_STEP = 16


def _dil_attn(q, k, v, batch, n_back, name):
    assert n_back <= BLOCK
    total_rows, width = q.shape
    length = total_rows // batch
    chunk_width = min(width, 4 * LANES)
    rows = min(length, BLOCK * DIL_ATTN_BLOCKS_PER_STEP * LANES // chunk_width)
    blocks_per_chunk = rows // BLOCK
    chunk_width = min(width, LANES * DIL_ATTN_BLOCKS_PER_STEP // blocks_per_chunk)
    view = lambda t: t.reshape(batch, length, width)
    cur_spec = pl.BlockSpec((1, rows, chunk_width), lambda bi, c, j: (bi, j, c))
    prev_spec = pl.BlockSpec((1, BLOCK, chunk_width),
                             lambda bi, c, j: (bi, jnp.maximum(j * blocks_per_chunk - 1, 0), c))
    o, lse = pl.pallas_call(
        functools.partial(_dil_attn_kernel, n_back=n_back),
        grid=(batch, width // chunk_width, length // rows),
        in_specs=[cur_spec, cur_spec, prev_spec, cur_spec, prev_spec],
        out_specs=[cur_spec, cur_spec],
        out_shape=[jax.ShapeDtypeStruct((batch, length, width), BF16),
                   jax.ShapeDtypeStruct((batch, length, width), F32)],
        compiler_params=_params("arbitrary", "arbitrary", "arbitrary"),
        name=name,
    )(view(q), view(k), view(k), view(v), view(v))
    return o.reshape(total_rows, width), lse.reshape(total_rows, width)


SB_FIRST_BLOCKS = 3
SB_TOP_ROWS = 32
SB_GROUP_HEADS = 4
SB_GROUP_WIDTH = SB_GROUP_HEADS * HEAD_DIM


def _suffix_sum_matrix(n_keys):
    j = jnp.arange(n_keys)[:, None]
    s = jnp.arange(n_keys)[None, :]
    u = (j > s).astype(BF16)
    return jnp.concatenate([u, u], axis=0)


def _stickbreak_kernel(q_ref, k_ref, v_ref, u_ref, o_ref, c_ref, acc_ref):
    step = pl.program_id(1)
    n_sub = q_ref.shape[1] // BLOCK
    n_groups = q_ref.shape[2] // SB_GROUP_WIDTH
    group_rows = SB_GROUP_HEADS * BLOCK
    group_cols = [slice(g * SB_GROUP_WIDTH, (g + 1) * SB_GROUP_WIDTH) for g in range(n_groups)]
    c_ref[...] = jnp.zeros_like(c_ref)
    acc_ref[...] = jnp.zeros_like(acc_ref)

    head_of_lane = lax.broadcasted_iota(jnp.int32, (BLOCK, SB_GROUP_WIDTH), 1) // HEAD_DIM

    def stack_heads(q):
        zero = jnp.zeros_like(q)
        return jnp.concatenate([jnp.where(head_of_lane == h, q, zero) for h in range(SB_GROUP_HEADS)], axis=0)

    q_stacked = {(sub, g): stack_heads(q_ref[0, sub * BLOCK:(sub + 1) * BLOCK, group_cols[g]])
                 for sub in range(n_sub) for g in range(n_groups)}
    def head_rows(g, r0, nr):
        return [slice(g * group_rows + h * BLOCK + r0, g * group_rows + h * BLOCK + r0 + nr)
                for h in range(SB_GROUP_HEADS)]

    def windows(jobs):
        units = [job + (g,) for job in jobs for g in range(n_groups)]

        scores = []
        for sub, k0, n_blocks, mask_from, r0, nr, g in units:
            qs = q_stacked[sub, g]
            if nr != BLOCK:
                qs = jnp.concatenate([qs[h * BLOCK + r0:h * BLOCK + r0 + nr] for h in range(SB_GROUP_HEADS)], axis=0)
            scores.append(_dot_nt(qs, k_ref[0, pl.ds(k0, n_blocks * BLOCK), group_cols[g]]))
        stage_b = []
        for (sub, k0, n_blocks, mask_from, r0, nr, g), z_all in zip(units, scores):
            if mask_from < n_blocks:
                row_iota = lax.broadcasted_iota(jnp.int32, (nr, BLOCK), 0)
                q_pos = (step * n_sub + sub) * BLOCK + r0 + jnp.concatenate([row_iota] * SB_GROUP_HEADS, axis=0)
                key_in_block = lax.broadcasted_iota(jnp.int32, (SB_GROUP_HEADS * nr, BLOCK), 1)
            log_betas, split, later = {}, {}, {}
            run = None
            for j in reversed(range(n_blocks)):
                z = z_all[:, j * BLOCK:(j + 1) * BLOCK]
                cost = jnp.maximum(z, 0.0) + jnp.log(1.0 + jnp.exp(-jnp.abs(z)))
                log_beta = z - cost
                if j >= mask_from:
                    valid = (k0 + j * BLOCK + key_in_block) < q_pos
                    cost = jnp.where(valid, cost, 0.0)
                    log_beta = jnp.where(valid, log_beta, MASKED_SCORE)
                hi = cost.astype(BF16)
                lo = (cost - hi.astype(F32)).astype(BF16)
                log_betas[j] = log_beta
                split[j] = jnp.concatenate([hi, lo], axis=1)
                later[j] = run
                tot = jnp.sum(cost, axis=1, keepdims=True)
                run = tot if run is None else run + tot
            within = _dot(jnp.concatenate([split[j] for j in range(n_blocks)], axis=0), u_ref[...])
            stage_b.append((log_betas, later, run, within))
        for (sub, k0, n_blocks, mask_from, r0, nr, g), (log_betas, later, total, within) in zip(units, stage_b):
            unit_rows = SB_GROUP_HEADS * nr
            c_old = jnp.concatenate([c_ref[sub, rows, :] for rows in head_rows(g, r0, nr)], axis=0)
            weights = []
            for j in range(n_blocks):
                base = c_old if later[j] is None else c_old + later[j]
                cost_after = base + within[j * unit_rows:(j + 1) * unit_rows]
                weights.append(jnp.exp(log_betas[j] - cost_after).astype(BF16))
            c_new = c_old + total
            for h, rows in enumerate(head_rows(g, r0, nr)):
                c_ref[sub, rows, :] = c_new[h * nr:(h + 1) * nr]
            pv = _dot(jnp.concatenate(weights, axis=1),
                      v_ref[0, pl.ds(k0, n_blocks * BLOCK), group_cols[g]])
            lane_head = lax.broadcasted_iota(jnp.int32, (nr, SB_GROUP_WIDTH), 1) // HEAD_DIM
            picked = pv[(SB_GROUP_HEADS - 1) * nr:]
            for h in reversed(range(SB_GROUP_HEADS - 1)):
                picked = jnp.where(lane_head == h, pv[h * nr:(h + 1) * nr], picked)
            acc_ref[sub, r0:r0 + nr, group_cols[g]] += picked

    def all_dead(sub, r0=0, nr=BLOCK):
        least = None
        for g in range(n_groups):
            for sl in head_rows(g, r0, nr):
                least = c_ref[sub, sl, :] if least is None else jnp.minimum(least, c_ref[sub, sl, :])
        return jnp.min(least) > -F32_EXP_ZERO_BELOW

    qi0 = step * n_sub
    top = SB_TOP_ROWS

    @pl.when(qi0 >= SB_FIRST_BLOCKS - 1)
    def _():
        firsts = [pl.multiple_of((qi0 + sub - (SB_FIRST_BLOCKS - 1)) * BLOCK, BLOCK) for sub in range(n_sub)]
        seconds = [pl.multiple_of((qi0 + sub - (SB_FIRST_BLOCKS - 2)) * BLOCK, BLOCK) for sub in range(n_sub)]
        near = SB_FIRST_BLOCKS - 1
        windows([(sub, seconds[sub], near, near - 1, 0, BLOCK) for sub in range(n_sub)]
                + [(sub, firsts[sub], 1, 1, 0, top) for sub in range(n_sub)])
        top_dead = [all_dead(sub, 0, top) for sub in range(n_sub)]
        rest_dead = [all_dead(sub, top, BLOCK - top) for sub in range(n_sub)]
        for sub in range(n_sub):
            @pl.when(jnp.logical_not(rest_dead[sub]))
            def _(sub=sub):
                windows([(sub, firsts[sub], 1, 1, top, BLOCK - top)])

            def body(carry, sub=sub):
                kb, _ = carry
                windows([(sub, pl.multiple_of(kb * BLOCK, BLOCK), 1, 1, 0, BLOCK)])
                done = jnp.logical_or(kb == 0, all_dead(sub))
                return kb - 1, done.astype(jnp.int32)

            first_block = qi0 + sub - (SB_FIRST_BLOCKS - 1)
            done_first = jnp.logical_or(first_block == 0, jnp.logical_and(top_dead[sub], rest_dead[sub]))
            lax.while_loop(lambda c: c[1] == 0, body, (first_block - 1, done_first.astype(jnp.int32)))

    @pl.when(qi0 < SB_FIRST_BLOCKS - 1)
    def _():
        windows([(sub, 0, SB_FIRST_BLOCKS - 1, 0, 0, BLOCK) for sub in range(n_sub)])

    for sub in range(n_sub):
        o_ref[0, sub * BLOCK:(sub + 1) * BLOCK, :] = acc_ref[sub].astype(o_ref.dtype)


SB_QUERY_BLOCKS_PER_STEP = 2


def _stickbreak(q, k, v):
    b, s, w = q.shape
    n_sub = SB_QUERY_BLOCKS_PER_STEP
    assert (SB_FIRST_BLOCKS - 1) % n_sub == 0 and s % (n_sub * BLOCK) == 0
    assert s >= SB_FIRST_BLOCKS * BLOCK and w % SB_GROUP_WIDTH == 0
    u = _suffix_sum_matrix(BLOCK)
    blk = pl.BlockSpec((1, n_sub * BLOCK, w), lambda bi, qi: (bi, qi, 0))
    full = pl.BlockSpec((1, s, w), lambda bi, qi: (bi, 0, 0))
    const = lambda arr: pl.BlockSpec(arr.shape, lambda bi, qi: (0, 0))
    n_heads = w // HEAD_DIM
    return pl.pallas_call(
        _stickbreak_kernel,
        grid=(b, s // (n_sub * BLOCK)),
        in_specs=[blk, full, full, const(u)],
        out_specs=blk,
        out_shape=jax.ShapeDtypeStruct((b, s, w), BF16),
        scratch_shapes=[pltpu.VMEM((n_sub, n_heads * BLOCK, LANES), F32), pltpu.VMEM((n_sub, BLOCK, w), F32)],
        compiler_params=_params("arbitrary", "arbitrary"),
        name="stickbreak",
    )(q, k, v, u)


MIX_ROWS = 512
MIX_ROW_PARTS = 2


def _mix_xattn_kernel(o0, o1, o2, l0, l1, l2, sb, ga, gb, x_ref, wa, wb, wo, gx, wxq, kv, wxo, out_ref,
                      so1, sl1, so2, sl2):
    rows = x_ref.shape[0]

    def undilate(ref, stage_ref, dilation):
        n_planes = A_GROUP_WIDTH // LANES
        for r in range(dilation):
            for c in range(n_planes):
                c0 = r * A_GROUP_WIDTH + c * LANES
                stage_ref[c, pl.ds(r, rows // dilation, stride=dilation), :] = ref[:, c0:c0 + LANES].astype(F32)
        return jnp.concatenate([stage_ref[c] for c in range(n_planes)], axis=1)

    dilations = [d for _, d in DIL_GROUPS]
    assert dilations[0] == 1
    ob, lb = undilate(o1, so1, dilations[1]), undilate(l1, sl1, dilations[1])
    oc, lc = undilate(o2, so2, dilations[2]), undilate(l2, sl2, dilations[2])

    part_rows = rows // MIX_ROW_PARTS
    parts = [slice(p * part_rows, (p + 1) * part_rows) for p in range(MIX_ROW_PARTS)]

    def mixed_residual(r):
        la = l0[r, :]
        m = jnp.maximum(jnp.maximum(la, lb[r]), lc[r])
        ea, eb, ec = jnp.exp(la - m), jnp.exp(lb[r] - m), jnp.exp(lc[r] - m)
        y = (ea * o0[r, :].astype(F32) + eb * ob[r] + ec * oc[r]) * (1.0 / (ea + eb + ec))
        ya = _dot(y.astype(BF16), wa[...])
        yb = _dot(sb[r, :], wb[...])
        mix = ga[r, :].astype(F32) * ya + gb[r, :].astype(F32) * yb
        return x_ref[r, :] + _dot(mix.astype(BF16), wo[...])

    def cross_attention(q):
        scale = X_HEAD_DIM ** -0.5
        heads = []
        for hd in range(X_HEADS):
            cols = slice(hd * X_HEAD_DIM, (hd + 1) * X_HEAD_DIM)
            kh = kv[0, :, cols]
            vh = kv[0, :, X_WIDTH + hd * X_HEAD_DIM:X_WIDTH + (hd + 1) * X_HEAD_DIM]
            s = _dot_nt(q[:, cols].astype(BF16), kh) * scale
            mx = jnp.max(s, axis=1, keepdims=True)
            p = jnp.exp(s - mx)
            l = jnp.sum(p, axis=1, keepdims=True)
            heads.append((_dot(p.astype(BF16), vh) * (1.0 / l)).astype(BF16))
        return jnp.concatenate(heads, axis=1)

    hs = [mixed_residual(r) for r in parts]
    qs = [_dot(_rms_norm_bf16(h, gx[...]), wxq[...]) for h in hs]
    os_ = [cross_attention(q) for q in qs]
    for r, h, o in zip(parts, hs, os_):
        out_ref[r, :] = h + _dot(o, wxo[...])


def _mix_xattn(o_groups, lse_groups, sb, ga, gb, x2, wa, wb, wo, gx, wxq, kv, wxo, seq):
    t_rows, d = x2.shape
    tm = MIX_ROWS
    tiles_per_seq = seq // tm
    row = lambda width: pl.BlockSpec((tm, width), lambda i: (i, 0))
    dil = lambda dilation: pl.BlockSpec((tm // dilation, dilation * A_GROUP_WIDTH), lambda i: (i, 0))
    const = lambda arr: pl.BlockSpec(arr.shape, lambda i: (0,) * arr.ndim)
    ml, kvw = kv.shape[1], kv.shape[2]
    group_specs = [dil(dilation) for _, dilation in DIL_GROUPS]
    return pl.pallas_call(
        _mix_xattn_kernel,
        grid=(t_rows // tm,),
        in_specs=group_specs + group_specs + [row(SB_WIDTH), row(d), row(d), row(d),
                  const(wa), const(wb), const(wo), const(gx), const(wxq),
                  pl.BlockSpec((1, ml, kvw), lambda i: (i // tiles_per_seq, 0, 0)),
                  const(wxo)],
        out_specs=row(d),
        out_shape=jax.ShapeDtypeStruct((t_rows, d), F32),
        scratch_shapes=[pltpu.VMEM((A_GROUP_WIDTH // LANES, tm, LANES), F32)] * 4,
        compiler_params=_params("arbitrary"),
        name="mix_xattn",
    )(*o_groups, *lse_groups, sb, ga, gb, x2, wa, wb, wo, gx, wxq, kv, wxo)


FFN_ROWS = 512
FFN_CHUNK = 256
HALO = SUBLANES


def _ffn_kernel(h_ref, g_ref, wup, cw_ref, cb_ref, wdn, gf_ref, out_ref, halo_ref, u_ref, act_ref,
                *, tiles_per_seq):
    i = pl.program_id(0)
    rows = h_ref.shape[0]
    d_ff = wdn.shape[0]

    @pl.when(i % tiles_per_seq == 0)
    def _():
        halo_ref[...] = jnp.zeros_like(halo_ref)

    h = h_ref[...]
    n = _rms_norm_bf16(h, g_ref[...])

    def conv(c0):
        cols = slice(c0, c0 + FFN_CHUNK)
        u = _dot(n, wup[:, cols])
        u_ref[0:HALO, :] = halo_ref[:, cols]
        u_ref[HALO:HALO + rows, :] = u
        halo_ref[:, cols] = u[rows - HALO:, :]
        out = cb_ref[:, cols] + cw_ref[CONV_WIDTH - 1:CONV_WIDTH, cols] * u
        for back in range(1, CONV_WIDTH):
            tap = CONV_WIDTH - 1 - back
            out = out + cw_ref[tap:tap + 1, cols] * u_ref[HALO - back:HALO - back + rows, :]
        return out

    for c in range(d_ff // FFN_CHUNK):
        gate = conv(c * FFN_CHUNK)
        val = conv(d_ff + c * FFN_CHUNK)
        act = gate * (1.0 / (1.0 + jnp.exp(-gate))) * val
        act_ref[:, c * FFN_CHUNK:(c + 1) * FFN_CHUNK] = act.astype(BF16)

    y = h + _dot(act_ref[...], wdn[...])
    ms = jnp.mean(y * y, axis=-1, keepdims=True)
    out_ref[...] = y * lax.rsqrt(ms + EPS) * gf_ref[...]


def _ffn(h2, g, wup, conv_w, conv_b, wdn, gf, seq):
    t_rows, d = h2.shape
    tm = FFN_ROWS
    d_ff = wdn.shape[0]
    assert d_ff % FFN_CHUNK == 0 and seq % tm == 0
    const = lambda arr: pl.BlockSpec(arr.shape, lambda i: (0,) * arr.ndim)
    return pl.pallas_call(
        functools.partial(_ffn_kernel, tiles_per_seq=seq // tm),
        grid=(t_rows // tm,),
        in_specs=[pl.BlockSpec((tm, d), lambda i: (i, 0)),
                  const(g), const(wup), const(conv_w), const(conv_b), const(wdn), const(gf)],
        out_specs=pl.BlockSpec((tm, d), lambda i: (i, 0)),
        out_shape=jax.ShapeDtypeStruct((t_rows, d), F32),
        scratch_shapes=[pltpu.VMEM((HALO, 2 * d_ff), F32),
                        pltpu.VMEM((HALO + tm, FFN_CHUNK), F32),
                        pltpu.VMEM((tm, d_ff), BF16)],
        compiler_params=_params("arbitrary"),
        name="ffn",
    )(h2, g, wup, conv_w, conv_b, wdn, gf)


def _rope_tables(seq):
    half = HEAD_DIM // 2
    inv_freq = 1.0 / (ROPE_THETA ** (jnp.arange(half, dtype=F32) * (2.0 / HEAD_DIM)))
    ang = jnp.arange(seq).astype(F32)[:, None] * inv_freq[None, :]
    cos, sin = jnp.cos(ang), jnp.sin(ang)
    reps = LANES // HEAD_DIM
    cos_t = jnp.tile(jnp.concatenate([cos, cos], axis=1), (1, reps))
    sin_t = jnp.tile(jnp.concatenate([-sin, sin], axis=1), (1, reps))
    return cos_t, sin_t


def kernel(x, mem, ln_mix_g, w_in, b_gate, w_branch_a, w_branch_b, w_out, ln_x_g, ln_mem_g, w_xq, w_xkv, w_xo,
           ln_ffn_g, w_up, conv_w, conv_b, w_down, ln_f_g):
    b, s, d = x.shape
    depth = w_in.shape[0]
    assert depth == 1, "the final norm is fused into the (single) layer's ffn kernel"
    cos_t, sin_t = _rope_tables(s)
    h = x.reshape(b * s, d)
    for l in range(depth):
        (qa0, qa1, qa2, ka0, ka1, ka2, va0, va1, va2, qb, kb, vb, ga, gb) = _in_proj(
            h, ln_mix_g[l][None], w_in[l].astype(BF16), b_gate[l][None], cos_t, sin_t, s)
        o_groups, lse_groups = [], []
        for g, ((window, dilation), q, k, v) in enumerate(zip(
                DIL_GROUPS, (qa0, qa1, qa2), (ka0, ka1, ka2), (va0, va1, va2))):
            o, lse = _dil_attn(q, k, v, b, window // dilation, f"dil_attn_g{g}")
            o_groups.append(o)
            lse_groups.append(lse)
        sb3 = (b, s, SB_WIDTH)
        sb = _stickbreak(qb.reshape(sb3), kb.reshape(sb3), vb.reshape(sb3)).reshape(b * s, SB_WIDTH)
        kv = _mem_kv(mem, ln_mem_g[l][None], w_xkv[l].astype(BF16))
        h = _mix_xattn(o_groups, lse_groups, sb, ga, gb, h,
                       w_branch_a[l].astype(BF16), w_branch_b[l].astype(BF16), w_out[l].astype(BF16),
                       ln_x_g[l][None], w_xq[l].astype(BF16), kv, w_xo[l].astype(BF16), s)
        h = _ffn(h, ln_ffn_g[l][None], w_up[l].astype(BF16), conv_w[l], conv_b[l][None],
                 w_down[l].astype(BF16), ln_f_g[None], s)
    return h.reshape(b, s, d)
```

```python
import functools

import jax
import jax.numpy as jnp
from jax import lax
from jax.experimental import pallas as pl
from jax.experimental.pallas import tpu as pltpu

F32 = jnp.float32
BF16 = jnp.bfloat16

LANES = 128
SUBLANES = 8
VMEM_LIMIT_BYTES = 56 * 1024 * 1024

HEAD_DIM = 64
DIL_GROUPS = ((128, 1), (512, 4), (2048, 16))
A_HEADS_PER_GROUP = 4
A_GROUP_WIDTH = A_HEADS_PER_GROUP * HEAD_DIM
SB_HEADS = 8
SB_WIDTH = SB_HEADS * HEAD_DIM
BLOCK = 128
ROPE_THETA = 10000.0
X_HEADS = 4
X_HEAD_DIM = 128
X_WIDTH = X_HEADS * X_HEAD_DIM
CONV_WIDTH = 3
EPS = 1e-6

F32_EXP_ZERO_BELOW = -104.0
MASKED_SCORE = -1e30


def _rms_norm_bf16(x, g):
    ms = jnp.mean(x * x, axis=-1, keepdims=True)
    return (x * lax.rsqrt(ms + EPS) * g).astype(BF16)


def _dot(a, b):
    return jnp.dot(a, b, preferred_element_type=F32)


def _dot_nt(a, b):
    return lax.dot_general(a, b, (((1,), (1,)), ((), ())), preferred_element_type=F32)


def _params(*semantics):
    return pltpu.CompilerParams(dimension_semantics=semantics, vmem_limit_bytes=VMEM_LIMIT_BYTES)


def _fetch_as_bf16(src_hbm, dst_ref, stage_ref, sems, chunks):
    copies = [pltpu.make_async_copy(src_hbm.at[idx], stage_ref.at[k % 2], sems.at[k % 2])
              for k, idx in enumerate(chunks)]
    for copy in copies[:2]:
        copy.start()
    for k, idx in enumerate(chunks):
        copies[k].wait()
        dst_ref[idx] = stage_ref[k % 2].astype(BF16)
        if k + 2 < len(chunks):
            copies[k + 2].start()


def _mem_kv_kernel(m_ref, g_ref, w_ref, o_ref):
    n = _rms_norm_bf16(m_ref[0], g_ref[...])
    o_ref[0] = _dot(n, w_ref[...]).astype(BF16)


def _mem_kv(mem, g, w_bf16):
    b, ml, d = mem.shape
    n_out = w_bf16.shape[1]
    return pl.pallas_call(
        _mem_kv_kernel,
        grid=(b,),
        in_specs=[pl.BlockSpec((1, ml, d), lambda i: (i, 0, 0)),
                  pl.BlockSpec((1, d), lambda i: (0, 0)),
                  pl.BlockSpec((d, n_out), lambda i: (0, 0))],
        out_specs=pl.BlockSpec((1, ml, n_out), lambda i: (i, 0, 0)),
        out_shape=jax.ShapeDtypeStruct((b, ml, n_out), BF16),
        compiler_params=_params("arbitrary"),
        name="mem_kv",
    )(mem, g, w_bf16)


IN_PROJ_ROWS = 1024


def _in_proj_kernel(x_ref, g_ref, w_hbm, bg_ref, cos_ref, sin_ref,
                    qa0, qa1, qa2, ka0, ka1, ka2, va0, va1, va2, qb, kb, vb, ga, gb,
                    stage_ref, w_ref, w_stage, sems):
    @pl.when(pl.program_id(0) == 0)
    def _():
        w_rows = w_stage.shape[1]
        _fetch_as_bf16(w_hbm, w_ref, w_stage, sems,
                       [(pl.ds(r * w_rows, w_rows), slice(None)) for r in range(w_ref.shape[0] // w_rows)])

    n = _rms_norm_bf16(x_ref[...], g_ref[...])
    rows = n.shape[0]
    cos = cos_ref[...]
    sin = sin_ref[...]
    lane = lax.broadcasted_iota(jnp.int32, (rows, LANES), 1)
    first_half = (lane % HEAD_DIM) < (HEAD_DIM // 2)

    def rope(t):
        partner = jnp.where(first_half, pltpu.roll(t, LANES - HEAD_DIM // 2, 1),
                            pltpu.roll(t, HEAD_DIM // 2, 1))
        return t * cos + partner * sin

    def proj(c0, width):
        return _dot(n, w_ref[:, c0:c0 + width])

    def store_dilated(ref, t, dilation):
        if dilation == 1:
            ref[...] = t.astype(BF16)
            return
        n_planes = A_GROUP_WIDTH // LANES
        for c in range(n_planes):
            stage_ref[c] = t[:, c * LANES:(c + 1) * LANES]
        for r in range(dilation):
            for c in range(n_planes):
                c0 = r * A_GROUP_WIDTH + c * LANES
                ref[:, c0:c0 + LANES] = stage_ref[c, pl.ds(r, rows // dilation, stride=dilation), :].astype(BF16)

    q_scale = HEAD_DIM ** -0.5
    dilations = [d for _, d in DIL_GROUPS]
    chunk = 256
    d_model = ga.shape[1]
    a_cols = 3 * len(DIL_GROUPS) * A_GROUP_WIDTH
    sb_cols = 3 * SB_WIDTH

    def window_attention_columns():
        col = 0
        for refs, roped, scale in (((qa0, qa1, qa2), True, q_scale), ((ka0, ka1, ka2), True, None),
                                   ((va0, va1, va2), False, None)):
            for ref, dilation in zip(refs, dilations):
                t = proj(col, A_GROUP_WIDTH)
                if roped:
                    t = jnp.concatenate([rope(t[:, h * LANES:(h + 1) * LANES])
                                         for h in range(A_GROUP_WIDTH // LANES)], axis=1)
                if scale is not None:
                    t = t * scale
                store_dilated(ref, t, dilation)
                col += A_GROUP_WIDTH

    def stickbreak_columns():
        col = a_cols
        for ref, scale in ((qb, q_scale), (kb, None), (vb, None)):
            for c in range(SB_WIDTH // chunk):
                t = proj(col, chunk)
                if scale is not None:
                    t = t * scale
                ref[:, c * chunk:(c + 1) * chunk] = t.astype(BF16)
                col += chunk

    def gate_columns():
        col, gcol = a_cols + sb_cols, 0
        for ref in (ga, gb):
            for c in range(d_model // chunk):
                t = proj(col, chunk) + bg_ref[:, gcol:gcol + chunk]
                ref[:, c * chunk:(c + 1) * chunk] = (1.0 / (1.0 + jnp.exp(-t))).astype(BF16)
                col += chunk
                gcol += chunk

    gate_columns()
    window_attention_columns()
    stickbreak_columns()


IN_PROJ_WEIGHT_CHUNK = 64


def _in_proj(x2, g, w_in, b_gate, cos_t, sin_t, seq):
    t_rows, d = x2.shape
    tm = IN_PROJ_ROWS
    n_cols = w_in.shape[1]
    assert d % IN_PROJ_WEIGHT_CHUNK == 0
    tiles_per_seq = seq // tm
    dilations = [dil for _, dil in DIL_GROUPS]
    outs = [(dil, dil * A_GROUP_WIDTH) for dil in dilations] * 3 + [(1, SB_WIDTH)] * 3 + [(1, d)] * 2
    return pl.pallas_call(
        _in_proj_kernel,
        grid=(t_rows // tm,),
        in_specs=[pl.BlockSpec((tm, d), lambda i: (i, 0)),
                  pl.BlockSpec((1, d), lambda i: (0, 0)),
                  pl.BlockSpec(memory_space=pl.ANY),
                  pl.BlockSpec((1, 2 * d), lambda i: (0, 0)),
                  pl.BlockSpec((tm, LANES), lambda i: (i % tiles_per_seq, 0)),
                  pl.BlockSpec((tm, LANES), lambda i: (i % tiles_per_seq, 0))],
        out_specs=[pl.BlockSpec((tm // div, w), lambda i: (i, 0)) for div, w in outs],
        out_shape=[jax.ShapeDtypeStruct((t_rows // div, w), BF16) for div, w in outs],
        scratch_shapes=[pltpu.VMEM((A_GROUP_WIDTH // LANES, tm, LANES), F32),
                        pltpu.VMEM((d, n_cols), BF16),
                        pltpu.VMEM((2, IN_PROJ_WEIGHT_CHUNK, n_cols), F32),
                        pltpu.SemaphoreType.DMA((2,))],
        compiler_params=_params("arbitrary"),
        name="in_proj",
    )(x2, g, w_in, b_gate, cos_t, sin_t)


def _dil_attn_kernel(q_ref, kc_ref, kp_ref, vc_ref, vp_ref, o_ref, lse_ref, *, n_back):
    j = pl.program_id(2)
    rows, width = q_ref.shape[1], q_ref.shape[2]
    row = lax.broadcasted_iota(jnp.int32, (2 * BLOCK, 2 * BLOCK), 0) % BLOCK
    col = lax.broadcasted_iota(jnp.int32, (2 * BLOCK, 2 * BLOCK), 1)
    dist = row - col + BLOCK
    band = (dist >= 0) & (dist <= n_back)
    first_col = jnp.where(j > 0, 0, BLOCK)
    band_first = band & (col >= first_col)
    lane = lax.broadcasted_iota(jnp.int32, (BLOCK, LANES), 1)
    head0 = lane < HEAD_DIM

    for cb in range(width // LANES):
        lanes = slice(cb * LANES, (cb + 1) * LANES)
        for sub in range(rows // BLOCK):
            cur = slice(sub * BLOCK, (sub + 1) * BLOCK)
            q = q_ref[0, cur, lanes]
            if sub == 0:
                k2 = jnp.concatenate([kp_ref[0, :, lanes], kc_ref[0, cur, lanes]], axis=0)
                v2 = jnp.concatenate([vp_ref[0, :, lanes], vc_ref[0, cur, lanes]], axis=0)
                mask = band_first
            else:
                both = slice((sub - 1) * BLOCK, (sub + 1) * BLOCK)
                k2 = kc_ref[0, both, lanes]
                v2 = vc_ref[0, both, lanes]
                mask = band
            zero = jnp.zeros_like(q)
            q2 = jnp.concatenate([jnp.where(head0, q, zero), jnp.where(head0, zero, q)], axis=0)
            s = jnp.where(mask, _dot_nt(q2, k2), MASKED_SCORE)
            m = jnp.max(s, axis=1, keepdims=True)
            p = jnp.exp(s - m)
            l = jnp.sum(p, axis=1, keepdims=True)
            o2 = _dot(p.astype(BF16), v2) * (1.0 / l)
            lse2 = m + jnp.log(l)
            o_ref[0, cur, lanes] = jnp.where(head0, o2[:BLOCK], o2[BLOCK:]).astype(o_ref.dtype)
            lse_ref[0, cur, lanes] = jnp.where(head0, lse2[:BLOCK], lse2[BLOCK:])


DIL_ATTN_BLOCKS_PER_STEP = 32


def _dil_attn(q, k, v, batch, n_back, name):
    assert n_back <= BLOCK
    total_rows, width = q.shape
    length = total_rows // batch
    chunk_width = min(width, 4 * LANES)
    rows = min(length, BLOCK * DIL_ATTN_BLOCKS_PER_STEP * LANES // chunk_width)
    blocks_per_chunk = rows // BLOCK
    chunk_width = min(width, LANES * DIL_ATTN_BLOCKS_PER_STEP // blocks_per_chunk)
    view = lambda t: t.reshape(batch, length, width)
    cur_spec = pl.BlockSpec((1, rows, chunk_width), lambda bi, c, j: (bi, j, c))
    prev_spec = pl.BlockSpec((1, BLOCK, chunk_width),
                             lambda bi, c, j: (bi, jnp.maximum(j * blocks_per_chunk - 1, 0), c))
    o, lse = pl.pallas_call(
        functools.partial(_dil_attn_kernel, n_back=n_back),
        grid=(batch, width // chunk_width, length // rows),
        in_specs=[cur_spec, cur_spec, prev_spec, cur_spec, prev_spec],
        out_specs=[cur_spec, cur_spec],
        out_shape=[jax.ShapeDtypeStruct((batch, length, width), BF16),
                   jax.ShapeDtypeStruct((batch, length, width), F32)],
        compiler_params=_params("arbitrary", "arbitrary", "arbitrary"),
        name=name,
    )(view(q), view(k), view(k), view(v), view(v))
    return o.reshape(total_rows, width), lse.reshape(total_rows, width)


SB_FIRST_BLOCKS = 3
SB_TOP_ROWS = 32
SB_GROUP_HEADS = 4
SB_GROUP_WIDTH = SB_GROUP_HEADS * HEAD_DIM


def _suffix_sum_matrix(n_keys):
    j = jnp.arange(n_keys)[:, None]
    s = jnp.arange(n_keys)[None, :]
    u = (j > s).astype(BF16)
    return jnp.concatenate([u, u], axis=0)


def _stickbreak_kernel(q_ref, k_ref, v_ref, u_ref, o_ref, c_ref, acc_ref):
    step = pl.program_id(1)
    n_sub = q_ref.shape[1] // BLOCK
    n_groups = q_ref.shape[2] // SB_GROUP_WIDTH
    group_rows = SB_GROUP_HEADS * BLOCK
    group_cols = [slice(g * SB_GROUP_WIDTH, (g + 1) * SB_GROUP_WIDTH) for g in range(n_groups)]
    head_of_lane = lax.broadcasted_iota(jnp.int32, (BLOCK, SB_GROUP_WIDTH), 1) // HEAD_DIM

    def stack_heads(q):
        zero = jnp.zeros_like(q)
        return jnp.concatenate([jnp.where(head_of_lane == h, q, zero) for h in range(SB_GROUP_HEADS)], axis=0)

    q_stacked = {(sub, g): stack_heads(q_ref[0, sub * BLOCK:(sub + 1) * BLOCK, group_cols[g]])
                 for sub in range(n_sub) for g in range(n_groups)}
    def head_rows(g, r0, nr):
        return [slice(g * group_rows + h * BLOCK + r0, g * group_rows + h * BLOCK + r0 + nr)
                for h in range(SB_GROUP_HEADS)]

    def windows(jobs):
        units = [job + (g,) for job in jobs for g in range(n_groups)]

        scores = []
        for sub, k0, n_blocks, mask_from, r0, nr, fresh, g in units:
            qs = q_stacked[sub, g]
            if nr != BLOCK:
                qs = jnp.concatenate([qs[h * BLOCK + r0:h * BLOCK + r0 + nr] for h in range(SB_GROUP_HEADS)], axis=0)
            scores.append(_dot_nt(qs, k_ref[0, pl.ds(k0, n_blocks * BLOCK), group_cols[g]]))
        stage_b = []
        for (sub, k0, n_blocks, mask_from, r0, nr, fresh, g), z_all in zip(units, scores):
            if mask_from < n_blocks:
                row_iota = lax.broadcasted_iota(jnp.int32, (nr, BLOCK), 0)
                q_pos = (step * n_sub + sub) * BLOCK + r0 + jnp.concatenate([row_iota] * SB_GROUP_HEADS, axis=0)
                key_in_block = lax.broadcasted_iota(jnp.int32, (SB_GROUP_HEADS * nr, BLOCK), 1)
            log_betas, split, later = {}, {}, {}
            run = None
            for j in reversed(range(n_blocks)):
                z = z_all[:, j * BLOCK:(j + 1) * BLOCK]
                cost = jnp.maximum(z, 0.0) + jnp.log(1.0 + jnp.exp(-jnp.abs(z)))
                log_beta = z - cost
                if j >= mask_from:
                    valid = (k0 + j * BLOCK + key_in_block) < q_pos
                    cost = jnp.where(valid, cost, 0.0)
                    log_beta = jnp.where(valid, log_beta, MASKED_SCORE)
                hi = cost.astype(BF16)
                lo = (cost - hi.astype(F32)).astype(BF16)
                log_betas[j] = log_beta
                split[j] = jnp.concatenate([hi, lo], axis=1)
                later[j] = run
                tot = jnp.sum(cost, axis=1, keepdims=True)
                run = tot if run is None else run + tot
            within = _dot(jnp.concatenate([split[j] for j in range(n_blocks)], axis=0), u_ref[...])
            stage_b.append((log_betas, later, run, within))
        for (sub, k0, n_blocks, mask_from, r0, nr, fresh, g), (log_betas, later, total, within) in zip(units, stage_b):
            unit_rows = SB_GROUP_HEADS * nr
            if fresh:
                c_old = None
            else:
                c_old = jnp.concatenate([c_ref[sub, rows, :] for rows in head_rows(g, r0, nr)], axis=0)
            weights = []
            for j in range(n_blocks):
                cost_after = within[j * unit_rows:(j + 1) * unit_rows]
                if later[j] is not None:
                    cost_after = cost_after + later[j]
                if c_old is not None:
                    cost_after = cost_after + c_old
                weights.append(jnp.exp(log_betas[j] - cost_after).astype(BF16))
            c_new = jnp.broadcast_to(total, (unit_rows, LANES)) if fresh else c_old + total
            for h, rows in enumerate(head_rows(g, r0, nr)):
                c_ref[sub, rows, :] = c_new[h * nr:(h + 1) * nr]
            pv = _dot(jnp.concatenate(weights, axis=1),
                      v_ref[0, pl.ds(k0, n_blocks * BLOCK), group_cols[g]])
            lane_head = lax.broadcasted_iota(jnp.int32, (nr, SB_GROUP_WIDTH), 1) // HEAD_DIM
            picked = pv[(SB_GROUP_HEADS - 1) * nr:]
            for h in reversed(range(SB_GROUP_HEADS - 1)):
                picked = jnp.where(lane_head == h, pv[h * nr:(h + 1) * nr], picked)
            if fresh:
                acc_ref[sub, r0:r0 + nr, group_cols[g]] = picked
            else:
                acc_ref[sub, r0:r0 + nr, group_cols[g]] += picked

    def all_dead(sub, r0=0, nr=BLOCK):
        least = None
        for g in range(n_groups):
            for sl in head_rows(g, r0, nr):
                least = c_ref[sub, sl, :] if least is None else jnp.minimum(least, c_ref[sub, sl, :])
        return jnp.min(least) > -F32_EXP_ZERO_BELOW

    qi0 = step * n_sub
    top = SB_TOP_ROWS

    @pl.when(qi0 >= SB_FIRST_BLOCKS - 1)
    def _():
        firsts = [pl.multiple_of((qi0 + sub - (SB_FIRST_BLOCKS - 1)) * BLOCK, BLOCK) for sub in range(n_sub)]
        seconds = [pl.multiple_of((qi0 + sub - (SB_FIRST_BLOCKS - 2)) * BLOCK, BLOCK) for sub in range(n_sub)]
        near = SB_FIRST_BLOCKS - 1
        windows([(sub, seconds[sub], near, near - 1, 0, BLOCK, True) for sub in range(n_sub)]
                + [(sub, firsts[sub], 1, 1, 0, top, False) for sub in range(n_sub)])
        top_dead = [all_dead(sub, 0, top) for sub in range(n_sub)]
        rest_dead = [all_dead(sub, top, BLOCK - top) for sub in range(n_sub)]
        for sub in range(n_sub):
            @pl.when(jnp.logical_not(rest_dead[sub]))
            def _(sub=sub):
                windows([(sub, firsts[sub], 1, 1, top, BLOCK - top, False)])

            def body(carry, sub=sub):
                kb, _ = carry
                windows([(sub, pl.multiple_of(kb * BLOCK, BLOCK), 1, 1, 0, BLOCK, False)])
                done = jnp.logical_or(kb == 0, all_dead(sub))
                return kb - 1, done.astype(jnp.int32)

            first_block = qi0 + sub - (SB_FIRST_BLOCKS - 1)
            done_first = jnp.logical_or(first_block == 0, jnp.logical_and(top_dead[sub], rest_dead[sub]))
            lax.while_loop(lambda c: c[1] == 0, body, (first_block - 1, done_first.astype(jnp.int32)))

    @pl.when(qi0 < SB_FIRST_BLOCKS - 1)
    def _():
        windows([(sub, 0, SB_FIRST_BLOCKS - 1, 0, 0, BLOCK, True) for sub in range(n_sub)])

    for sub in range(n_sub):
        o_ref[0, sub * BLOCK:(sub + 1) * BLOCK, :] = acc_ref[sub].astype(o_ref.dtype)


SB_QUERY_BLOCKS_PER_STEP = 2


def _stickbreak(q, k, v):
    b, s, w = q.shape
    n_sub = SB_QUERY_BLOCKS_PER_STEP
    assert (SB_FIRST_BLOCKS - 1) % n_sub == 0 and s % (n_sub * BLOCK) == 0
    assert s >= SB_FIRST_BLOCKS * BLOCK and w % SB_GROUP_WIDTH == 0
    u = _suffix_sum_matrix(BLOCK)
    blk = pl.BlockSpec((1, n_sub * BLOCK, w), lambda bi, qi: (bi, qi, 0))
    full = pl.BlockSpec((1, s, w), lambda bi, qi: (bi, 0, 0))
    const = lambda arr: pl.BlockSpec(arr.shape, lambda bi, qi: (0, 0))
    n_heads = w // HEAD_DIM
    return pl.pallas_call(
        _stickbreak_kernel,
        grid=(b, s // (n_sub * BLOCK)),
        in_specs=[blk, full, full, const(u)],
        out_specs=blk,
        out_shape=jax.ShapeDtypeStruct((b, s, w), BF16),
        scratch_shapes=[pltpu.VMEM((n_sub, n_heads * BLOCK, LANES), F32), pltpu.VMEM((n_sub, BLOCK, w), F32)],
        compiler_params=_params("arbitrary", "arbitrary"),
        name="stickbreak",
    )(q, k, v, u)


MIX_ROWS = 1024
MIX_ROW_PARTS = 4


def _mix_xattn_kernel(o0, o1, o2, l0, l1, l2, sb, ga, gb, x_ref, wa, wb, wo, gx, wxq, kv, wxo, out_ref,
                      so1, sl1, so2, sl2):
    rows = x_ref.shape[0]

    def undilate(ref, stage_ref, dilation):
        n_planes = A_GROUP_WIDTH // LANES
        for r in range(dilation):
            for c in range(n_planes):
                c0 = r * A_GROUP_WIDTH + c * LANES
                stage_ref[c, pl.ds(r, rows // dilation, stride=dilation), :] = ref[:, c0:c0 + LANES].astype(F32)
        return jnp.concatenate([stage_ref[c] for c in range(n_planes)], axis=1)

    dilations = [d for _, d in DIL_GROUPS]
    assert dilations[0] == 1
    ob, lb = undilate(o1, so1, dilations[1]), undilate(l1, sl1, dilations[1])
    oc, lc = undilate(o2, so2, dilations[2]), undilate(l2, sl2, dilations[2])

    part_rows = rows // MIX_ROW_PARTS
    parts = [slice(p * part_rows, (p + 1) * part_rows) for p in range(MIX_ROW_PARTS)]

    def mixed_residual(r):
        yb = _dot(sb[r, :], wb[...])
        la = l0[r, :]
        m = jnp.maximum(jnp.maximum(la, lb[r]), lc[r])
        ea, eb, ec = jnp.exp(la - m), jnp.exp(lb[r] - m), jnp.exp(lc[r] - m)
        y = (ea * o0[r, :].astype(F32) + eb * ob[r] + ec * oc[r]) * (1.0 / (ea + eb + ec))
        ya = _dot(y.astype(BF16), wa[...])
        mix = ga[r, :].astype(F32) * ya + gb[r, :].astype(F32) * yb
        return x_ref[r, :] + _dot(mix.astype(BF16), wo[...])

    def cross_attention(q):
        scale = X_HEAD_DIM ** -0.5
        heads = []
        for hd in range(X_HEADS):
            cols = slice(hd * X_HEAD_DIM, (hd + 1) * X_HEAD_DIM)
            kh = kv[0, :, cols]
            vh = kv[0, :, X_WIDTH + hd * X_HEAD_DIM:X_WIDTH + (hd + 1) * X_HEAD_DIM]
            s = _dot_nt(q[:, cols].astype(BF16), kh) * scale
            mx = jnp.max(s, axis=1, keepdims=True)
            p = jnp.exp(s - mx)
            l = jnp.sum(p, axis=1, keepdims=True)
            heads.append((_dot(p.astype(BF16), vh) * (1.0 / l)).astype(BF16))
        return jnp.concatenate(heads, axis=1)

    hs = [mixed_residual(r) for r in parts]
    qs = [_dot(_rms_norm_bf16(h, gx[...]), wxq[...]) for h in hs]
    os_ = [cross_attention(q) for q in qs]
    for r, h, o in zip(parts, hs, os_):
        out_ref[r, :] = h + _dot(o, wxo[...])


def _mix_xattn(o_groups, lse_groups, sb, ga, gb, x2, wa, wb, wo, gx, wxq, kv, wxo, seq):
    t_rows, d = x2.shape
    tm = MIX_ROWS
    tiles_per_seq = seq // tm
    row = lambda width: pl.BlockSpec((tm, width), lambda i: (i, 0))
    dil = lambda dilation: pl.BlockSpec((tm // dilation, dilation * A_GROUP_WIDTH), lambda i: (i, 0))
    const = lambda arr: pl.BlockSpec(arr.shape, lambda i: (0,) * arr.ndim)
    ml, kvw = kv.shape[1], kv.shape[2]
    group_specs = [dil(dilation) for _, dilation in DIL_GROUPS]
    return pl.pallas_call(
        _mix_xattn_kernel,
        grid=(t_rows // tm,),
        in_specs=group_specs + group_specs + [row(SB_WIDTH), row(d), row(d), row(d),
                  const(wa), const(wb), const(wo), const(gx), const(wxq),
                  pl.BlockSpec((1, ml, kvw), lambda i: (i // tiles_per_seq, 0, 0)),
                  const(wxo)],
        out_specs=row(d),
        out_shape=jax.ShapeDtypeStruct((t_rows, d), F32),
        scratch_shapes=[pltpu.VMEM((A_GROUP_WIDTH // LANES, tm, LANES), F32)] * 4,
        compiler_params=_params("arbitrary"),
        name="mix_xattn",
    )(*o_groups, *lse_groups, sb, ga, gb, x2, wa, wb, wo, gx, wxq, kv, wxo)


FFN_ROWS = 512
FFN_CHUNK = 256
HALO = SUBLANES


def _ffn_kernel(h_ref, g_ref, wup_hbm, cw_ref, cb_ref, wdn_hbm, gf_ref, out_ref,
                halo_ref, u_ref, act_ref, wup, wdn, up_stage, dn_stage, sems, *, tiles_per_seq):
    i = pl.program_id(0)
    rows = h_ref.shape[0]
    d_ff = wdn.shape[0]

    @pl.when(i == 0)
    def _():
        up_rows = up_stage.shape[1]
        _fetch_as_bf16(wup_hbm, wup, up_stage, sems,
                       [(pl.ds(r * up_rows, up_rows), slice(None)) for r in range(wup.shape[0] // up_rows)])
        dn_rows = dn_stage.shape[1]
        _fetch_as_bf16(wdn_hbm, wdn, dn_stage, sems,
                       [(pl.ds(r * dn_rows, dn_rows), slice(None)) for r in range(d_ff // dn_rows)])

    @pl.when(i % tiles_per_seq == 0)
    def _():
        halo_ref[...] = jnp.zeros_like(halo_ref)

    h = h_ref[...]
    n = _rms_norm_bf16(h, g_ref[...])

    def conv(c0):
        cols = slice(c0, c0 + FFN_CHUNK)
        u = _dot(n, wup[:, cols])
        u_ref[0:HALO, :] = halo_ref[:, cols]
        u_ref[HALO:HALO + rows, :] = u
        halo_ref[:, cols] = u[rows - HALO:, :]
        out = cb_ref[:, cols] + cw_ref[CONV_WIDTH - 1:CONV_WIDTH, cols] * u
        for back in range(1, CONV_WIDTH):
            tap = CONV_WIDTH - 1 - back
            out = out + cw_ref[tap:tap + 1, cols] * u_ref[HALO - back:HALO - back + rows, :]
        return out

    for c in range(d_ff // FFN_CHUNK):
        gate = conv(c * FFN_CHUNK)
        val = conv(d_ff + c * FFN_CHUNK)
        act = gate * (1.0 / (1.0 + jnp.exp(-gate))) * val
        act_ref[:, c * FFN_CHUNK:(c + 1) * FFN_CHUNK] = act.astype(BF16)

    y = h + _dot(act_ref[...], wdn[...])
    ms = jnp.mean(y * y, axis=-1, keepdims=True)
    out_ref[...] = y * lax.rsqrt(ms + EPS) * gf_ref[...]


FFN_UP_WEIGHT_CHUNK = 128
FFN_DOWN_WEIGHT_CHUNK = 256


def _ffn(h2, g, wup, conv_w, conv_b, wdn, gf, seq):
    t_rows, d = h2.shape
    tm = FFN_ROWS
    d_ff = wdn.shape[0]
    assert d_ff % FFN_CHUNK == 0 and seq % tm == 0
    assert wup.shape[0] % FFN_UP_WEIGHT_CHUNK == 0 and d_ff % FFN_DOWN_WEIGHT_CHUNK == 0
    const = lambda arr: pl.BlockSpec(arr.shape, lambda i: (0,) * arr.ndim)
    in_hbm = pl.BlockSpec(memory_space=pl.ANY)
    return pl.pallas_call(
        functools.partial(_ffn_kernel, tiles_per_seq=seq // tm),
        grid=(t_rows // tm,),
        in_specs=[pl.BlockSpec((tm, d), lambda i: (i, 0)),
                  const(g), in_hbm, const(conv_w), const(conv_b), in_hbm, const(gf)],
        out_specs=pl.BlockSpec((tm, d), lambda i: (i, 0)),
        out_shape=jax.ShapeDtypeStruct((t_rows, d), F32),
        scratch_shapes=[pltpu.VMEM((HALO, 2 * d_ff), F32),
                        pltpu.VMEM((HALO + tm, FFN_CHUNK), F32),
                        pltpu.VMEM((tm, d_ff), BF16),
                        pltpu.VMEM(wup.shape, BF16),
                        pltpu.VMEM(wdn.shape, BF16),
                        pltpu.VMEM((2, FFN_UP_WEIGHT_CHUNK, wup.shape[1]), F32),
                        pltpu.VMEM((2, FFN_DOWN_WEIGHT_CHUNK, wdn.shape[1]), F32),
                        pltpu.SemaphoreType.DMA((2,))],
        compiler_params=_params("arbitrary"),
        name="ffn",
    )(h2, g, wup, conv_w, conv_b, wdn, gf)


def _rope_tables(seq):
    half = HEAD_DIM // 2
    inv_freq = 1.0 / (ROPE_THETA ** (jnp.arange(half, dtype=F32) * (2.0 / HEAD_DIM)))
    ang = jnp.arange(seq).astype(F32)[:, None] * inv_freq[None, :]
    cos, sin = jnp.cos(ang), jnp.sin(ang)
    reps = LANES // HEAD_DIM
    cos_t = jnp.tile(jnp.concatenate([cos, cos], axis=1), (1, reps))
    sin_t = jnp.tile(jnp.concatenate([-sin, sin], axis=1), (1, reps))
    return cos_t, sin_t


def kernel(x, mem, ln_mix_g, w_in, b_gate, w_branch_a, w_branch_b, w_out, ln_x_g, ln_mem_g, w_xq, w_xkv, w_xo,
           ln_ffn_g, w_up, conv_w, conv_b, w_down, ln_f_g):
    b, s, d = x.shape
    depth = w_in.shape[0]
    assert depth == 1, "the final norm is fused into the (single) layer's ffn kernel"
    cos_t, sin_t = _rope_tables(s)
    h = x.reshape(b * s, d)
    for l in range(depth):
        (qa0, qa1, qa2, ka0, ka1, ka2, va0, va1, va2, qb, kb, vb, ga, gb) = _in_proj(
            h, ln_mix_g[l][None], w_in[l], b_gate[l][None], cos_t, sin_t, s)
        o_groups, lse_groups = [], []
        for g, ((window, dilation), q, k, v) in enumerate(zip(
                DIL_GROUPS, (qa0, qa1, qa2), (ka0, ka1, ka2), (va0, va1, va2))):
            o, lse = _dil_attn(q, k, v, b, window // dilation, f"dil_attn_g{g}")
            o_groups.append(o)
            lse_groups.append(lse)
        sb3 = (b, s, SB_WIDTH)
        sb = _stickbreak(qb.reshape(sb3), kb.reshape(sb3), vb.reshape(sb3)).reshape(b * s, SB_WIDTH)
        kv = _mem_kv(mem, ln_mem_g[l][None], w_xkv[l].astype(BF16))
        h = _mix_xattn(o_groups, lse_groups, sb, ga, gb, h,
                       w_branch_a[l].astype(BF16), w_branch_b[l].astype(BF16), w_out[l].astype(BF16),
                       ln_x_g[l][None], w_xq[l].astype(BF16), kv, w_xo[l].astype(BF16), s)
        h = _ffn(h, ln_ffn_g[l][None], w_up[l], conv_w[l], conv_b[l][None], w_down[l], ln_f_g[None], s)
    return h.reshape(b, s, d)
```

```python
import functools

import jax
import jax.numpy as jnp
import numpy as np
from jax import lax
from jax.experimental import pallas as pl
from jax.experimental.pallas import tpu as pltpu

F32 = jnp.float32
BF16 = jnp.bfloat16

LANES = 128
SUBLANES = 8
VMEM_LIMIT_BYTES = 56 * 1024 * 1024

HEAD_DIM = 64
DIL_GROUPS = ((128, 1), (512, 4), (2048, 16))
A_HEADS_PER_GROUP = 4
A_GROUP_WIDTH = A_HEADS_PER_GROUP * HEAD_DIM
SB_HEADS = 8
SB_WIDTH = SB_HEADS * HEAD_DIM
BLOCK = 128
ROPE_THETA = 10000.0
X_HEADS = 4
X_HEAD_DIM = 128
X_WIDTH = X_HEADS * X_HEAD_DIM
CONV_WIDTH = 3
EPS = 1e-6

F32_EXP_ZERO_BELOW = -104.0
MASKED_SCORE = -1e30


def _rms_norm_bf16(x, g):
    ms = jnp.mean(x * x, axis=-1, keepdims=True)
    return (x * lax.rsqrt(ms + EPS) * g).astype(BF16)


def _dot(a, b):
    return jnp.dot(a, b, preferred_element_type=F32)


def _dot_nt(a, b):
    return lax.dot_general(a, b, (((1,), (1,)), ((), ())), preferred_element_type=F32)


def _params(*semantics):
    return pltpu.CompilerParams(dimension_semantics=semantics, vmem_limit_bytes=VMEM_LIMIT_BYTES)


def _fetch_as_bf16(src_hbm, dst_ref, stage_ref, sems, chunks):
    copies = [pltpu.make_async_copy(src_hbm.at[idx], stage_ref.at[k % 2], sems.at[k % 2])
              for k, idx in enumerate(chunks)]
    for copy in copies[:2]:
        copy.start()
    for k, idx in enumerate(chunks):
        copies[k].wait()
        dst_ref[idx] = stage_ref[k % 2].astype(BF16)
        if k + 2 < len(chunks):
            copies[k + 2].start()


def _mem_kv_kernel(m_ref, g_ref, w_ref, o_ref):
    n = _rms_norm_bf16(m_ref[0], g_ref[...])
    o_ref[0] = _dot(n, w_ref[...]).astype(BF16)


def _mem_kv(mem, g, w_bf16):
    b, ml, d = mem.shape
    n_out = w_bf16.shape[1]
    return pl.pallas_call(
        _mem_kv_kernel,
        grid=(b,),
        in_specs=[pl.BlockSpec((1, ml, d), lambda i: (i, 0, 0)),
                  pl.BlockSpec((1, d), lambda i: (0, 0)),
                  pl.BlockSpec((d, n_out), lambda i: (0, 0))],
        out_specs=pl.BlockSpec((1, ml, n_out), lambda i: (i, 0, 0)),
        out_shape=jax.ShapeDtypeStruct((b, ml, n_out), BF16),
        compiler_params=_params("arbitrary"),
        name="mem_kv",
    )(mem, g, w_bf16)


IN_PROJ_ROWS = 1024


def _in_proj_kernel(x_ref, g_ref, w_hbm, bg_ref, cos_ref, sin_ref,
                    qa0, qa1, qa2, ka0, ka1, ka2, va0, va1, va2, qb, kb, vb, ga, gb,
                    stage_ref, w_ref, w_stage, sems):
    @pl.when(pl.program_id(0) == 0)
    def _():
        w_rows = w_stage.shape[1]
        _fetch_as_bf16(w_hbm, w_ref, w_stage, sems,
                       [(pl.ds(r * w_rows, w_rows), slice(None)) for r in range(w_ref.shape[0] // w_rows)])

    n = _rms_norm_bf16(x_ref[...], g_ref[...])
    rows = n.shape[0]
    cos = cos_ref[...]
    sin = sin_ref[...]
    lane = lax.broadcasted_iota(jnp.int32, (rows, LANES), 1)
    first_half = (lane % HEAD_DIM) < (HEAD_DIM // 2)

    def rope(t):
        partner = jnp.where(first_half, pltpu.roll(t, LANES - HEAD_DIM // 2, 1),
                            pltpu.roll(t, HEAD_DIM // 2, 1))
        return t * cos + partner * sin

    def proj(c0, width):
        return _dot(n, w_ref[:, c0:c0 + width])

    def store_dilated(ref, t, dilation):
        if dilation == 1:
            ref[...] = t.astype(BF16)
            return
        n_planes = A_GROUP_WIDTH // LANES
        for c in range(n_planes):
            stage_ref[c] = t[:, c * LANES:(c + 1) * LANES]
        for r in range(dilation):
            for c in range(n_planes):
                c0 = r * A_GROUP_WIDTH + c * LANES
                ref[:, c0:c0 + LANES] = stage_ref[c, pl.ds(r, rows // dilation, stride=dilation), :].astype(BF16)

    q_scale = HEAD_DIM ** -0.5
    dilations = [d for _, d in DIL_GROUPS]
    chunk = 256
    d_model = ga.shape[1]
    a_cols = 3 * len(DIL_GROUPS) * A_GROUP_WIDTH
    sb_cols = 3 * SB_WIDTH

    def window_attention_columns():
        col = 0
        for refs, roped, scale in (((qa0, qa1, qa2), True, q_scale), ((ka0, ka1, ka2), True, None),
                                   ((va0, va1, va2), False, None)):
            for ref, dilation in zip(refs, dilations):
                t = proj(col, A_GROUP_WIDTH)
                if roped:
                    t = jnp.concatenate([rope(t[:, h * LANES:(h + 1) * LANES])
                                         for h in range(A_GROUP_WIDTH // LANES)], axis=1)
                if scale is not None:
                    t = t * scale
                store_dilated(ref, t, dilation)
                col += A_GROUP_WIDTH

    def stickbreak_columns():
        col = a_cols
        for ref, scale in ((qb, q_scale), (kb, None), (vb, None)):
            for c in range(SB_WIDTH // chunk):
                t = proj(col, chunk)
                if scale is not None:
                    t = t * scale
                ref[:, c * chunk:(c + 1) * chunk] = t.astype(BF16)
                col += chunk

    def gate_columns():
        col, gcol = a_cols + sb_cols, 0
        for ref in (ga, gb):
            for c in range(d_model // chunk):
                t = proj(col, chunk) + bg_ref[:, gcol:gcol + chunk]
                ref[:, c * chunk:(c + 1) * chunk] = (1.0 / (1.0 + jnp.exp(-t))).astype(BF16)
                col += chunk
                gcol += chunk

    gate_columns()
    window_attention_columns()
    stickbreak_columns()


IN_PROJ_WEIGHT_CHUNK = 64


def _in_proj(x2, g, w_in, b_gate, cos_t, sin_t, seq):
    t_rows, d = x2.shape
    tm = IN_PROJ_ROWS
    n_cols = w_in.shape[1]
    assert d % IN_PROJ_WEIGHT_CHUNK == 0
    tiles_per_seq = seq // tm
    dilations = [dil for _, dil in DIL_GROUPS]
    outs = [(dil, dil * A_GROUP_WIDTH) for dil in dilations] * 3 + [(1, SB_WIDTH)] * 3 + [(1, d)] * 2
    return pl.pallas_call(
        _in_proj_kernel,
        grid=(t_rows // tm,),
        in_specs=[pl.BlockSpec((tm, d), lambda i: (i, 0)),
                  pl.BlockSpec((1, d), lambda i: (0, 0)),
                  pl.BlockSpec(memory_space=pl.ANY),
                  pl.BlockSpec((1, 2 * d), lambda i: (0, 0)),
                  pl.BlockSpec((tm, LANES), lambda i: (i % tiles_per_seq, 0)),
                  pl.BlockSpec((tm, LANES), lambda i: (i % tiles_per_seq, 0))],
        out_specs=[pl.BlockSpec((tm // div, w), lambda i: (i, 0)) for div, w in outs],
        out_shape=[jax.ShapeDtypeStruct((t_rows // div, w), BF16) for div, w in outs],
        scratch_shapes=[pltpu.VMEM((A_GROUP_WIDTH // LANES, tm, LANES), F32),
                        pltpu.VMEM((d, n_cols), BF16),
                        pltpu.VMEM((2, IN_PROJ_WEIGHT_CHUNK, n_cols), F32),
                        pltpu.SemaphoreType.DMA((2,))],
        compiler_params=_params("arbitrary"),
        name="in_proj",
    )(x2, g, w_in, b_gate, cos_t, sin_t)


def _dil_attn_kernel(q_ref, kc_ref, kp_ref, vc_ref, vp_ref, o_ref, lse_ref, *, n_back):
    j = pl.program_id(2)
    rows, width = q_ref.shape[1], q_ref.shape[2]
    row = lax.broadcasted_iota(jnp.int32, (2 * BLOCK, 2 * BLOCK), 0) % BLOCK
    col = lax.broadcasted_iota(jnp.int32, (2 * BLOCK, 2 * BLOCK), 1)
    dist = row - col + BLOCK
    band = (dist >= 0) & (dist <= n_back)
    first_col = jnp.where(j > 0, 0, BLOCK)
    band_first = band & (col >= first_col)
    lane = lax.broadcasted_iota(jnp.int32, (BLOCK, LANES), 1)
    head0 = lane < HEAD_DIM

    for cb in range(width // LANES):
        lanes = slice(cb * LANES, (cb + 1) * LANES)
        for sub in range(rows // BLOCK):
            cur = slice(sub * BLOCK, (sub + 1) * BLOCK)
            q = q_ref[0, cur, lanes]
            if sub == 0:
                k2 = jnp.concatenate([kp_ref[0, :, lanes], kc_ref[0, cur, lanes]], axis=0)
                v2 = jnp.concatenate([vp_ref[0, :, lanes], vc_ref[0, cur, lanes]], axis=0)
                mask = band_first
            else:
                both = slice((sub - 1) * BLOCK, (sub + 1) * BLOCK)
                k2 = kc_ref[0, both, lanes]
                v2 = vc_ref[0, both, lanes]
                mask = band
            zero = jnp.zeros_like(q)
            q2 = jnp.concatenate([jnp.where(head0, q, zero), jnp.where(head0, zero, q)], axis=0)
            s = jnp.where(mask, _dot_nt(q2, k2), MASKED_SCORE)
            m = jnp.max(s, axis=1, keepdims=True)
            p = jnp.exp(s - m)
            l = jnp.sum(p, axis=1, keepdims=True)
            o2 = _dot(p.astype(BF16), v2) * (1.0 / l)
            lse2 = m + jnp.log(l)
            o_ref[0, cur, lanes] = jnp.where(head0, o2[:BLOCK], o2[BLOCK:]).astype(o_ref.dtype)
            lse_ref[0, cur, lanes] = jnp.where(head0, lse2[:BLOCK], lse2[BLOCK:])


DIL_ATTN_BLOCKS_PER_STEP = 32


def _dil_attn(q, k, v, batch, n_back, name):
    assert n_back <= BLOCK
    total_rows, width = q.shape
    length = total_rows // batch
    chunk_width = min(width, 4 * LANES)
    rows = min(length, BLOCK * DIL_ATTN_BLOCKS_PER_STEP * LANES // chunk_width)
    blocks_per_chunk = rows // BLOCK
    chunk_width = min(width, LANES * DIL_ATTN_BLOCKS_PER_STEP // blocks_per_chunk)
    view = lambda t: t.reshape(batch, length, width)
    cur_spec = pl.BlockSpec((1, rows, chunk_width), lambda bi, c, j: (bi, j, c))
    prev_spec = pl.BlockSpec((1, BLOCK, chunk_width),
                             lambda bi, c, j: (bi, jnp.maximum(j * blocks_per_chunk - 1, 0), c))
    o, lse = pl.pallas_call(
        functools.partial(_dil_attn_kernel, n_back=n_back),
        grid=(batch, width // chunk_width, length // rows),
        in_specs=[cur_spec, cur_spec, prev_spec, cur_spec, prev_spec],
        out_specs=[cur_spec, cur_spec],
        out_shape=[jax.ShapeDtypeStruct((batch, length, width), BF16),
                   jax.ShapeDtypeStruct((batch, length, width), F32)],
        compiler_params=_params("arbitrary", "arbitrary", "arbitrary"),
        name=name,
    )(view(q), view(k), view(k), view(v), view(v))
    return o.reshape(total_rows, width), lse.reshape(total_rows, width)


SB_FIRST_BLOCKS = 3
SB_TOP_ROWS = 32
SB_GROUP_HEADS = 4
SB_GROUP_WIDTH = SB_GROUP_HEADS * HEAD_DIM


def _suffix_sum_matrix(n_keys):
    j = np.arange(n_keys)[:, None]
    s = np.arange(n_keys)[None, :]
    u = (j > s).astype(np.float32)
    return jnp.asarray(np.concatenate([u, u], axis=0), BF16)


def _stickbreak_kernel(q_ref, k_ref, v_ref, u_ref, o_ref, c_ref, acc_ref):
    step = pl.program_id(1)
    n_sub = q_ref.shape[1] // BLOCK
    n_groups = q_ref.shape[2] // SB_GROUP_WIDTH
    group_rows = SB_GROUP_HEADS * BLOCK
    group_cols = [slice(g * SB_GROUP_WIDTH, (g + 1) * SB_GROUP_WIDTH) for g in range(n_groups)]
    head_of_lane = lax.broadcasted_iota(jnp.int32, (BLOCK, SB_GROUP_WIDTH), 1) // HEAD_DIM

    def stack_heads(q):
        zero = jnp.zeros_like(q)
        return jnp.concatenate([jnp.where(head_of_lane == h, q, zero) for h in range(SB_GROUP_HEADS)], axis=0)

    q_stacked = {(sub, g): stack_heads(q_ref[0, sub * BLOCK:(sub + 1) * BLOCK, group_cols[g]])
                 for sub in range(n_sub) for g in range(n_groups)}
    def head_rows(g, r0, nr):
        return [slice(g * group_rows + h * BLOCK + r0, g * group_rows + h * BLOCK + r0 + nr)
                for h in range(SB_GROUP_HEADS)]

    def windows(jobs):
        units = [job + (g,) for job in jobs for g in range(n_groups)]

        scores = []
        for sub, k0, n_blocks, mask_from, r0, nr, fresh, g in units:
            qs = q_stacked[sub, g]
            if nr != BLOCK:
                qs = jnp.concatenate([qs[h * BLOCK + r0:h * BLOCK + r0 + nr] for h in range(SB_GROUP_HEADS)], axis=0)
            scores.append(_dot_nt(qs, k_ref[0, pl.ds(k0, n_blocks * BLOCK), group_cols[g]]))
        stage_b = []
        for (sub, k0, n_blocks, mask_from, r0, nr, fresh, g), z_all in zip(units, scores):
            if mask_from < n_blocks:
                row_iota = lax.broadcasted_iota(jnp.int32, (nr, BLOCK), 0)
                q_pos = (step * n_sub + sub) * BLOCK + r0 + jnp.concatenate([row_iota] * SB_GROUP_HEADS, axis=0)
                key_in_block = lax.broadcasted_iota(jnp.int32, (SB_GROUP_HEADS * nr, BLOCK), 1)
            log_betas, split, later = {}, {}, {}
            run = None
            for j in reversed(range(n_blocks)):
                z = z_all[:, j * BLOCK:(j + 1) * BLOCK]
                cost = jnp.maximum(z, 0.0) + jnp.log(1.0 + jnp.exp(-jnp.abs(z)))
                log_beta = z - cost
                if j >= mask_from:
                    valid = (k0 + j * BLOCK + key_in_block) < q_pos
                    cost = jnp.where(valid, cost, 0.0)
                    log_beta = jnp.where(valid, log_beta, MASKED_SCORE)
                hi = cost.astype(BF16)
                lo = (cost - hi.astype(F32)).astype(BF16)
                log_betas[j] = log_beta
                split[j] = jnp.concatenate([hi, lo], axis=1)
                later[j] = run
                tot = jnp.sum(cost, axis=1, keepdims=True)
                run = tot if run is None else run + tot
            within = _dot(jnp.concatenate([split[j] for j in range(n_blocks)], axis=0), u_ref[...])
            stage_b.append((log_betas, later, run, within))
        for (sub, k0, n_blocks, mask_from, r0, nr, fresh, g), (log_betas, later, total, within) in zip(units, stage_b):
            unit_rows = SB_GROUP_HEADS * nr
            if fresh:
                c_old = None
            else:
                c_old = jnp.concatenate([c_ref[sub, rows, :] for rows in head_rows(g, r0, nr)], axis=0)
            weights = []
            for j in range(n_blocks):
                cost_after = within[j * unit_rows:(j + 1) * unit_rows]
                if later[j] is not None:
                    cost_after = cost_after + later[j]
                if c_old is not None:
                    cost_after = cost_after + c_old
                weights.append(jnp.exp(log_betas[j] - cost_after).astype(BF16))
            c_new = jnp.broadcast_to(total, (unit_rows, LANES)) if fresh else c_old + total
            for h, rows in enumerate(head_rows(g, r0, nr)):
                c_ref[sub, rows, :] = c_new[h * nr:(h + 1) * nr]
            pv = _dot(jnp.concatenate(weights, axis=1),
                      v_ref[0, pl.ds(k0, n_blocks * BLOCK), group_cols[g]])
            lane_head = lax.broadcasted_iota(jnp.int32, (nr, SB_GROUP_WIDTH), 1) // HEAD_DIM
            picked = pv[(SB_GROUP_HEADS - 1) * nr:]
            for h in reversed(range(SB_GROUP_HEADS - 1)):
                picked = jnp.where(lane_head == h, pv[h * nr:(h + 1) * nr], picked)
            if fresh:
                acc_ref[sub, r0:r0 + nr, group_cols[g]] = picked
            else:
                acc_ref[sub, r0:r0 + nr, group_cols[g]] += picked

    def all_dead(sub, r0=0, nr=BLOCK):
        least = None
        for g in range(n_groups):
            for sl in head_rows(g, r0, nr):
                least = c_ref[sub, sl, :] if least is None else jnp.minimum(least, c_ref[sub, sl, :])
        return jnp.min(least) > -F32_EXP_ZERO_BELOW

    qi0 = step * n_sub
    top = SB_TOP_ROWS

    near = SB_FIRST_BLOCKS - 1

    def key_start(block):
        return block * BLOCK if isinstance(block, int) else pl.multiple_of(block * BLOCK, BLOCK)

    def run_blocks(query_blocks):
        first_jobs, top_jobs, walkers = [], [], []
        for sub, qb in enumerate(query_blocks):
            if isinstance(qb, int) and qb < near:
                first_jobs.append((sub, 0, qb + 1, qb, 0, BLOCK, True))
            else:
                first_jobs.append((sub, key_start(qb - near + 1), near, near - 1, 0, BLOCK, True))
                top_jobs.append((sub, key_start(qb - near), 1, 1, 0, top, False))
                walkers.append((sub, qb - near))
        windows(first_jobs + top_jobs)
        top_dead = {sub: all_dead(sub, 0, top) for sub, _ in walkers}
        rest_dead = {sub: all_dead(sub, top, BLOCK - top) for sub, _ in walkers}
        for sub, far_block in walkers:
            @pl.when(jnp.logical_not(rest_dead[sub]))
            def _(sub=sub, far_block=far_block):
                windows([(sub, key_start(far_block), 1, 1, top, BLOCK - top, False)])

            if isinstance(far_block, int) and far_block == 0:
                continue

            def body(carry, sub=sub):
                kb, _ = carry
                windows([(sub, key_start(kb), 1, 1, 0, BLOCK, False)])
                done = jnp.logical_or(kb == 0, all_dead(sub))
                return kb - 1, done.astype(jnp.int32)

            done_first = jnp.logical_or(far_block == 0, jnp.logical_and(top_dead[sub], rest_dead[sub]))
            lax.while_loop(lambda c: c[1] == 0, body,
                           (jnp.asarray(far_block - 1, jnp.int32), done_first.astype(jnp.int32)))

    @pl.when(step > 0)
    def _():
        run_blocks([qi0 + sub for sub in range(n_sub)])

    @pl.when(step == 0)
    def _():
        run_blocks(list(range(n_sub)))

    for sub in range(n_sub):
        o_ref[0, sub * BLOCK:(sub + 1) * BLOCK, :] = acc_ref[sub].astype(o_ref.dtype)


SB_QUERY_BLOCKS_PER_STEP = 4


def _stickbreak(q, k, v):
    b, s, w = q.shape
    n_sub = SB_QUERY_BLOCKS_PER_STEP
    assert n_sub >= SB_FIRST_BLOCKS - 1 and s % (n_sub * BLOCK) == 0
    assert s >= SB_FIRST_BLOCKS * BLOCK and w % SB_GROUP_WIDTH == 0
    u = _suffix_sum_matrix(BLOCK)
    blk = pl.BlockSpec((1, n_sub * BLOCK, w), lambda bi, qi: (bi, qi, 0))
    full = pl.BlockSpec((1, s, w), lambda bi, qi: (bi, 0, 0))
    const = lambda arr: pl.BlockSpec(arr.shape, lambda bi, qi: (0, 0))
    n_heads = w // HEAD_DIM
    return pl.pallas_call(
        _stickbreak_kernel,
        grid=(b, s // (n_sub * BLOCK)),
        in_specs=[blk, full, full, const(u)],
        out_specs=blk,
        out_shape=jax.ShapeDtypeStruct((b, s, w), BF16),
        scratch_shapes=[pltpu.VMEM((n_sub, n_heads * BLOCK, LANES), F32), pltpu.VMEM((n_sub, BLOCK, w), F32)],
        compiler_params=_params("arbitrary", "arbitrary"),
        name="stickbreak",
    )(q, k, v, u)


MIX_ROWS = 1024
MIX_ROW_PARTS = 4


def _mix_xattn_kernel(o0, o1, o2, l0, l1, l2, sb, ga, gb, x_ref, wa, wb, wo, gx, wxq, kv, wxo, out_ref,
                      so1, sl1, so2, sl2):
    rows = x_ref.shape[0]

    def undilate(ref, stage_ref, dilation):
        n_planes = A_GROUP_WIDTH // LANES
        for r in range(dilation):
            for c in range(n_planes):
                c0 = r * A_GROUP_WIDTH + c * LANES
                stage_ref[c, pl.ds(r, rows // dilation, stride=dilation), :] = ref[:, c0:c0 + LANES].astype(F32)
        return jnp.concatenate([stage_ref[c] for c in range(n_planes)], axis=1)

    dilations = [d for _, d in DIL_GROUPS]
    assert dilations[0] == 1
    ob, lb = undilate(o1, so1, dilations[1]), undilate(l1, sl1, dilations[1])
    oc, lc = undilate(o2, so2, dilations[2]), undilate(l2, sl2, dilations[2])

    part_rows = rows // MIX_ROW_PARTS
    parts = [slice(p * part_rows, (p + 1) * part_rows) for p in range(MIX_ROW_PARTS)]

    def mixed_residual(r):
        yb = _dot(sb[r, :], wb[...])
        la = l0[r, :]
        m = jnp.maximum(jnp.maximum(la, lb[r]), lc[r])
        ea, eb, ec = jnp.exp(la - m), jnp.exp(lb[r] - m), jnp.exp(lc[r] - m)
        y = (ea * o0[r, :].astype(F32) + eb * ob[r] + ec * oc[r]) * (1.0 / (ea + eb + ec))
        ya = _dot(y.astype(BF16), wa[...])
        mix = ga[r, :].astype(F32) * ya + gb[r, :].astype(F32) * yb
        return x_ref[r, :] + _dot(mix.astype(BF16), wo[...])

    def cross_attention(q):
        scale = X_HEAD_DIM ** -0.5
        heads = []
        for hd in range(X_HEADS):
            cols = slice(hd * X_HEAD_DIM, (hd + 1) * X_HEAD_DIM)
            kh = kv[0, :, cols]
            vh = kv[0, :, X_WIDTH + hd * X_HEAD_DIM:X_WIDTH + (hd + 1) * X_HEAD_DIM]
            s = _dot_nt(q[:, cols].astype(BF16), kh) * scale
            mx = jnp.max(s, axis=1, keepdims=True)
            p = jnp.exp(s - mx)
            l = jnp.sum(p, axis=1, keepdims=True)
            heads.append((_dot(p.astype(BF16), vh) * (1.0 / l)).astype(BF16))
        return jnp.concatenate(heads, axis=1)

    hs = [mixed_residual(r) for r in parts]
    qs = [_dot(_rms_norm_bf16(h, gx[...]), wxq[...]) for h in hs]
    os_ = [cross_attention(q) for q in qs]
    for r, h, o in zip(parts, hs, os_):
        out_ref[r, :] = h + _dot(o, wxo[...])


def _mix_xattn(o_groups, lse_groups, sb, ga, gb, x2, wa, wb, wo, gx, wxq, kv, wxo, seq):
    t_rows, d = x2.shape
    tm = MIX_ROWS
    tiles_per_seq = seq // tm
    row = lambda width: pl.BlockSpec((tm, width), lambda i: (i, 0))
    dil = lambda dilation: pl.BlockSpec((tm // dilation, dilation * A_GROUP_WIDTH), lambda i: (i, 0))
    const = lambda arr: pl.BlockSpec(arr.shape, lambda i: (0,) * arr.ndim)
    ml, kvw = kv.shape[1], kv.shape[2]
    group_specs = [dil(dilation) for _, dilation in DIL_GROUPS]
    return pl.pallas_call(
        _mix_xattn_kernel,
        grid=(t_rows // tm,),
        in_specs=group_specs + group_specs + [row(SB_WIDTH), row(d), row(d), row(d),
                  const(wa), const(wb), const(wo), const(gx), const(wxq),
                  pl.BlockSpec((1, ml, kvw), lambda i: (i // tiles_per_seq, 0, 0)),
                  const(wxo)],
        out_specs=row(d),
        out_shape=jax.ShapeDtypeStruct((t_rows, d), F32),
        scratch_shapes=[pltpu.VMEM((A_GROUP_WIDTH // LANES, tm, LANES), F32)] * 4,
        compiler_params=_params("arbitrary"),
        name="mix_xattn",
    )(*o_groups, *lse_groups, sb, ga, gb, x2, wa, wb, wo, gx, wxq, kv, wxo)


FFN_ROWS = 512
FFN_CHUNK = 256
HALO = SUBLANES


def _ffn_kernel(h_ref, g_ref, wup_hbm, cw_ref, cb_ref, wdn_hbm, gf_ref, out_ref,
                halo_ref, u_ref, act_ref, wup, wdn, up_stage, dn_stage, sems, *, tiles_per_seq):
    i = pl.program_id(0)
    rows = h_ref.shape[0]
    d_ff = wdn.shape[0]

    @pl.when(i == 0)
    def _():
        up_rows = up_stage.shape[1]
        _fetch_as_bf16(wup_hbm, wup, up_stage, sems,
                       [(pl.ds(r * up_rows, up_rows), slice(None)) for r in range(wup.shape[0] // up_rows)])
        dn_rows = dn_stage.shape[1]
        _fetch_as_bf16(wdn_hbm, wdn, dn_stage, sems,
                       [(pl.ds(r * dn_rows, dn_rows), slice(None)) for r in range(d_ff // dn_rows)])

    @pl.when(i % tiles_per_seq == 0)
    def _():
        halo_ref[...] = jnp.zeros_like(halo_ref)

    h = h_ref[...]
    n = _rms_norm_bf16(h, g_ref[...])

    def conv(c0):
        cols = slice(c0, c0 + FFN_CHUNK)
        u = _dot(n, wup[:, cols])
        u_ref[0:HALO, :] = halo_ref[:, cols]
        u_ref[HALO:HALO + rows, :] = u
        halo_ref[:, cols] = u[rows - HALO:, :]
        out = cb_ref[:, cols] + cw_ref[CONV_WIDTH - 1:CONV_WIDTH, cols] * u
        for back in range(1, CONV_WIDTH):
            tap = CONV_WIDTH - 1 - back
            out = out + cw_ref[tap:tap + 1, cols] * u_ref[HALO - back:HALO - back + rows, :]
        return out

    for c in range(d_ff // FFN_CHUNK):
        gate = conv(c * FFN_CHUNK)
        val = conv(d_ff + c * FFN_CHUNK)
        act = gate * (1.0 / (1.0 + jnp.exp(-gate))) * val
        act_ref[:, c * FFN_CHUNK:(c + 1) * FFN_CHUNK] = act.astype(BF16)

    y = h + _dot(act_ref[...], wdn[...])
    ms = jnp.mean(y * y, axis=-1, keepdims=True)
    out_ref[...] = y * lax.rsqrt(ms + EPS) * gf_ref[...]


FFN_UP_WEIGHT_CHUNK = 128
FFN_DOWN_WEIGHT_CHUNK = 256


def _ffn(h2, g, wup, conv_w, conv_b, wdn, gf, seq):
    t_rows, d = h2.shape
    tm = FFN_ROWS
    d_ff = wdn.shape[0]
    assert d_ff % FFN_CHUNK == 0 and seq % tm == 0
    assert wup.shape[0] % FFN_UP_WEIGHT_CHUNK == 0 and d_ff % FFN_DOWN_WEIGHT_CHUNK == 0
    const = lambda arr: pl.BlockSpec(arr.shape, lambda i: (0,) * arr.ndim)
    in_hbm = pl.BlockSpec(memory_space=pl.ANY)
    return pl.pallas_call(
        functools.partial(_ffn_kernel, tiles_per_seq=seq // tm),
        grid=(t_rows // tm,),
        in_specs=[pl.BlockSpec((tm, d), lambda i: (i, 0)),
                  const(g), in_hbm, const(conv_w), const(conv_b), in_hbm, const(gf)],
        out_specs=pl.BlockSpec((tm, d), lambda i: (i, 0)),
        out_shape=jax.ShapeDtypeStruct((t_rows, d), F32),
        scratch_shapes=[pltpu.VMEM((HALO, 2 * d_ff), F32),
                        pltpu.VMEM((HALO + tm, FFN_CHUNK), F32),
                        pltpu.VMEM((tm, d_ff), BF16),
                        pltpu.VMEM(wup.shape, BF16),
                        pltpu.VMEM(wdn.shape, BF16),
                        pltpu.VMEM((2, FFN_UP_WEIGHT_CHUNK, wup.shape[1]), F32),
                        pltpu.VMEM((2, FFN_DOWN_WEIGHT_CHUNK, wdn.shape[1]), F32),
                        pltpu.SemaphoreType.DMA((2,))],
        compiler_params=_params("arbitrary"),
        name="ffn",
    )(h2, g, wup, conv_w, conv_b, wdn, gf)


def _rope_tables(seq):
    half = HEAD_DIM // 2
    f32 = np.float32
    inv_freq = f32(1.0) / (f32(ROPE_THETA) ** (np.arange(half, dtype=f32) * f32(2.0 / HEAD_DIM)))
    ang = np.arange(seq, dtype=f32)[:, None] * inv_freq[None, :]
    cos, sin = np.cos(ang), np.sin(ang)
    reps = LANES // HEAD_DIM
    cos_t = np.tile(np.concatenate([cos, cos], axis=1), (1, reps))
    sin_t = np.tile(np.concatenate([-sin, sin], axis=1), (1, reps))
    return jnp.asarray(cos_t, F32), jnp.asarray(sin_t, F32)


def kernel(x, mem, ln_mix_g, w_in, b_gate, w_branch_a, w_branch_b, w_out, ln_x_g, ln_mem_g, w_xq, w_xkv, w_xo,
           ln_ffn_g, w_up, conv_w, conv_b, w_down, ln_f_g):
    b, s, d = x.shape
    depth = w_in.shape[0]
    assert depth == 1, "the final norm is fused into the (single) layer's ffn kernel"
    cos_t, sin_t = _rope_tables(s)
    h = x.reshape(b * s, d)
    for l in range(depth):
        (qa0, qa1, qa2, ka0, ka1, ka2, va0, va1, va2, qb, kb, vb, ga, gb) = _in_proj(
            h, ln_mix_g[l][None], w_in[l], b_gate[l][None], cos_t, sin_t, s)
        o_groups, lse_groups = [], []
        for g, ((window, dilation), q, k, v) in enumerate(zip(
                DIL_GROUPS, (qa0, qa1, qa2), (ka0, ka1, ka2), (va0, va1, va2))):
            o, lse = _dil_attn(q, k, v, b, window // dilation, f"dil_attn_g{g}")
            o_groups.append(o)
            lse_groups.append(lse)
        sb3 = (b, s, SB_WIDTH)
        sb = _stickbreak(qb.reshape(sb3), kb.reshape(sb3), vb.reshape(sb3)).reshape(b * s, SB_WIDTH)
        kv = _mem_kv(mem, ln_mem_g[l][None], w_xkv[l].astype(BF16))
        h = _mix_xattn(o_groups, lse_groups, sb, ga, gb, h,
                       w_branch_a[l].astype(BF16), w_branch_b[l].astype(BF16), w_out[l].astype(BF16),
                       ln_x_g[l][None], w_xq[l].astype(BF16), kv, w_xo[l].astype(BF16), s)
        h = _ffn(h, ln_ffn_g[l][None], w_up[l], conv_w[l], conv_b[l][None], w_down[l], ln_f_g[None], s)
    return h.reshape(b, s, d)
```

```python
import functools

import jax
import jax.numpy as jnp
import numpy as np
from jax import lax
from jax.experimental import pallas as pl
from jax.experimental.pallas import tpu as pltpu

F32 = jnp.float32
BF16 = jnp.bfloat16

LANES = 128
SUBLANES = 8
VMEM_LIMIT_BYTES = 56 * 1024 * 1024

HEAD_DIM = 64
DIL_GROUPS = ((128, 1), (512, 4), (2048, 16))
A_HEADS_PER_GROUP = 4
A_GROUP_WIDTH = A_HEADS_PER_GROUP * HEAD_DIM
SB_HEADS = 8
SB_WIDTH = SB_HEADS * HEAD_DIM
BLOCK = 128
ROPE_THETA = 10000.0
X_HEADS = 4
X_HEAD_DIM = 128
X_WIDTH = X_HEADS * X_HEAD_DIM
CONV_WIDTH = 3
EPS = 1e-6

F32_EXP_ZERO_BELOW = -104.0
MASKED_SCORE = -1e30


def _rms_norm_bf16(x, g):
    ms = jnp.mean(x * x, axis=-1, keepdims=True)
    return (x * lax.rsqrt(ms + EPS) * g).astype(BF16)


def _dot(a, b):
    return jnp.dot(a, b, preferred_element_type=F32)


def _dot_nt(a, b):
    return lax.dot_general(a, b, (((1,), (1,)), ((), ())), preferred_element_type=F32)


def _params(*semantics):
    return pltpu.CompilerParams(dimension_semantics=semantics, vmem_limit_bytes=VMEM_LIMIT_BYTES)


def _fetch_as_bf16(src_hbm, dst_ref, stage_ref, sems, chunks):
    copies = [pltpu.make_async_copy(src_hbm.at[idx], stage_ref.at[k % 2], sems.at[k % 2])
              for k, idx in enumerate(chunks)]
    for copy in copies[:2]:
        copy.start()
    for k, idx in enumerate(chunks):
        copies[k].wait()
        dst_ref[idx] = stage_ref[k % 2].astype(BF16)
        if k + 2 < len(chunks):
            copies[k + 2].start()


def _mem_kv_kernel(m_ref, g_ref, w_ref, o_ref):
    n = _rms_norm_bf16(m_ref[0], g_ref[...])
    o_ref[0] = _dot(n, w_ref[...]).astype(BF16)


def _mem_kv(mem, g, w_bf16):
    b, ml, d = mem.shape
    n_out = w_bf16.shape[1]
    return pl.pallas_call(
        _mem_kv_kernel,
        grid=(b,),
        in_specs=[pl.BlockSpec((1, ml, d), lambda i: (i, 0, 0)),
                  pl.BlockSpec((1, d), lambda i: (0, 0)),
                  pl.BlockSpec((d, n_out), lambda i: (0, 0))],
        out_specs=pl.BlockSpec((1, ml, n_out), lambda i: (i, 0, 0)),
        out_shape=jax.ShapeDtypeStruct((b, ml, n_out), BF16),
        compiler_params=_params("arbitrary"),
        name="mem_kv",
    )(mem, g, w_bf16)


IN_PROJ_ROWS = 1024


def _in_proj_kernel(x_ref, g_ref, w_hbm, bg_ref, cos_ref, sin_ref,
                    qa0, qa1, qa2, ka0, ka1, ka2, va0, va1, va2, qb, kb, vb, ga, gb,
                    stage_ref, w_ref, w_stage, sems):
    @pl.when(pl.program_id(0) == 0)
    def _():
        w_rows = w_stage.shape[1]
        _fetch_as_bf16(w_hbm, w_ref, w_stage, sems,
                       [(pl.ds(r * w_rows, w_rows), slice(None)) for r in range(w_ref.shape[0] // w_rows)])

    n = _rms_norm_bf16(x_ref[...], g_ref[...])
    rows = n.shape[0]
    cos = cos_ref[...]
    sin = sin_ref[...]
    lane = lax.broadcasted_iota(jnp.int32, (rows, LANES), 1)
    first_half = (lane % HEAD_DIM) < (HEAD_DIM // 2)

    def rope(t):
        partner = jnp.where(first_half, pltpu.roll(t, LANES - HEAD_DIM // 2, 1),
                            pltpu.roll(t, HEAD_DIM // 2, 1))
        return t * cos + partner * sin

    def proj(c0, width):
        return _dot(n, w_ref[:, c0:c0 + width])

    def store_dilated(ref, t, dilation):
        if dilation == 1:
            ref[...] = t.astype(BF16)
            return
        n_planes = A_GROUP_WIDTH // LANES
        for c in range(n_planes):
            stage_ref[c] = t[:, c * LANES:(c + 1) * LANES]
        for r in range(dilation):
            for c in range(n_planes):
                c0 = r * A_GROUP_WIDTH + c * LANES
                ref[:, c0:c0 + LANES] = stage_ref[c, pl.ds(r, rows // dilation, stride=dilation), :].astype(BF16)

    q_scale = HEAD_DIM ** -0.5
    dilations = [d for _, d in DIL_GROUPS]
    chunk = 256
    d_model = ga.shape[1]
    a_cols = 3 * len(DIL_GROUPS) * A_GROUP_WIDTH
    sb_cols = 3 * SB_WIDTH

    def window_attention_columns():
        col = 0
        for refs, roped, scale in (((qa0, qa1, qa2), True, q_scale), ((ka0, ka1, ka2), True, None),
                                   ((va0, va1, va2), False, None)):
            for ref, dilation in zip(refs, dilations):
                t = proj(col, A_GROUP_WIDTH)
                if roped:
                    t = jnp.concatenate([rope(t[:, h * LANES:(h + 1) * LANES])
                                         for h in range(A_GROUP_WIDTH // LANES)], axis=1)
                if scale is not None:
                    t = t * scale
                store_dilated(ref, t, dilation)
                col += A_GROUP_WIDTH

    def stickbreak_columns():
        col = a_cols
        for ref, scale in ((qb, q_scale), (kb, None), (vb, None)):
            for c in range(SB_WIDTH // chunk):
                t = proj(col, chunk)
                if scale is not None:
                    t = t * scale
                ref[:, c * chunk:(c + 1) * chunk] = t.astype(BF16)
                col += chunk

    def gate_columns():
        col, gcol = a_cols + sb_cols, 0
        for ref in (ga, gb):
            for c in range(d_model // chunk):
                t = proj(col, chunk) + bg_ref[:, gcol:gcol + chunk]
                ref[:, c * chunk:(c + 1) * chunk] = (1.0 / (1.0 + jnp.exp(-t))).astype(BF16)
                col += chunk
                gcol += chunk

    gate_columns()
    window_attention_columns()
    stickbreak_columns()


IN_PROJ_WEIGHT_CHUNK = 64


def _in_proj(x2, g, w_in, b_gate, cos_t, sin_t, seq):
    t_rows, d = x2.shape
    tm = IN_PROJ_ROWS
    n_cols = w_in.shape[1]
    assert d % IN_PROJ_WEIGHT_CHUNK == 0
    tiles_per_seq = seq // tm
    dilations = [dil for _, dil in DIL_GROUPS]
    outs = [(dil, dil * A_GROUP_WIDTH) for dil in dilations] * 3 + [(1, SB_WIDTH)] * 3 + [(1, d)] * 2
    return pl.pallas_call(
        _in_proj_kernel,
        grid=(t_rows // tm,),
        in_specs=[pl.BlockSpec((tm, d), lambda i: (i, 0)),
                  pl.BlockSpec((1, d), lambda i: (0, 0)),
                  pl.BlockSpec(memory_space=pl.ANY),
                  pl.BlockSpec((1, 2 * d), lambda i: (0, 0)),
                  pl.BlockSpec((tm, LANES), lambda i: (i % tiles_per_seq, 0)),
                  pl.BlockSpec((tm, LANES), lambda i: (i % tiles_per_seq, 0))],
        out_specs=[pl.BlockSpec((tm // div, w), lambda i: (i, 0)) for div, w in outs],
        out_shape=[jax.ShapeDtypeStruct((t_rows // div, w), BF16) for div, w in outs],
        scratch_shapes=[pltpu.VMEM((A_GROUP_WIDTH // LANES, tm, LANES), F32),
                        pltpu.VMEM((d, n_cols), BF16),
                        pltpu.VMEM((2, IN_PROJ_WEIGHT_CHUNK, n_cols), F32),
                        pltpu.SemaphoreType.DMA((2,))],
        compiler_params=_params("arbitrary"),
        name="in_proj",
    )(x2, g, w_in, b_gate, cos_t, sin_t)


def _dil_attn_kernel(q_ref, kc_ref, kp_ref, vc_ref, vp_ref, o_ref, lse_ref, *, n_back):
    j = pl.program_id(2)
    rows, width = q_ref.shape[1], q_ref.shape[2]
    row = lax.broadcasted_iota(jnp.int32, (2 * BLOCK, 2 * BLOCK), 0) % BLOCK
    col = lax.broadcasted_iota(jnp.int32, (2 * BLOCK, 2 * BLOCK), 1)
    dist = row - col + BLOCK
    band = (dist >= 0) & (dist <= n_back)
    first_col = jnp.where(j > 0, 0, BLOCK)
    band_first = band & (col >= first_col)
    lane = lax.broadcasted_iota(jnp.int32, (BLOCK, LANES), 1)
    head0 = lane < HEAD_DIM

    for cb in range(width // LANES):
        lanes = slice(cb * LANES, (cb + 1) * LANES)
        for sub in range(rows // BLOCK):
            cur = slice(sub * BLOCK, (sub + 1) * BLOCK)
            q = q_ref[0, cur, lanes]
            if sub == 0:
                k2 = jnp.concatenate([kp_ref[0, :, lanes], kc_ref[0, cur, lanes]], axis=0)
                v2 = jnp.concatenate([vp_ref[0, :, lanes], vc_ref[0, cur, lanes]], axis=0)
                mask = band_first
            else:
                both = slice((sub - 1) * BLOCK, (sub + 1) * BLOCK)
                k2 = kc_ref[0, both, lanes]
                v2 = vc_ref[0, both, lanes]
                mask = band
            zero = jnp.zeros_like(q)
            q2 = jnp.concatenate([jnp.where(head0, q, zero), jnp.where(head0, zero, q)], axis=0)
            s = jnp.where(mask, _dot_nt(q2, k2), MASKED_SCORE)
            m = jnp.max(s, axis=1, keepdims=True)
            p = jnp.exp(s - m)
            l = jnp.sum(p, axis=1, keepdims=True)
            o2 = _dot(p.astype(BF16), v2) * (1.0 / l)
            lse2 = m + jnp.log(l)
            o_ref[0, cur, lanes] = jnp.where(head0, o2[:BLOCK], o2[BLOCK:]).astype(o_ref.dtype)
            lse_ref[0, cur, lanes] = jnp.where(head0, lse2[:BLOCK], lse2[BLOCK:])


DIL_ATTN_BLOCKS_PER_STEP = 32


def _dil_attn(q, k, v, batch, n_back, name):
    assert n_back <= BLOCK
    total_rows, width = q.shape
    length = total_rows // batch
    chunk_width = min(width, 4 * LANES)
    rows = min(length, BLOCK * DIL_ATTN_BLOCKS_PER_STEP * LANES // chunk_width)
    blocks_per_chunk = rows // BLOCK
    chunk_width = min(width, LANES * DIL_ATTN_BLOCKS_PER_STEP // blocks_per_chunk)
    view = lambda t: t.reshape(batch, length, width)
    cur_spec = pl.BlockSpec((1, rows, chunk_width), lambda bi, c, j: (bi, j, c))
    prev_spec = pl.BlockSpec((1, BLOCK, chunk_width),
                             lambda bi, c, j: (bi, jnp.maximum(j * blocks_per_chunk - 1, 0), c))
    o, lse = pl.pallas_call(
        functools.partial(_dil_attn_kernel, n_back=n_back),
        grid=(batch, width // chunk_width, length // rows),
        in_specs=[cur_spec, cur_spec, prev_spec, cur_spec, prev_spec],
        out_specs=[cur_spec, cur_spec],
        out_shape=[jax.ShapeDtypeStruct((batch, length, width), BF16),
                   jax.ShapeDtypeStruct((batch, length, width), F32)],
        compiler_params=_params("arbitrary", "arbitrary", "arbitrary"),
        name=name,
    )(view(q), view(k), view(k), view(v), view(v))
    return o.reshape(total_rows, width), lse.reshape(total_rows, width)


SB_FIRST_BLOCKS = 3
SB_TOP_ROWS = 32
SB_GROUP_HEADS = 4
SB_GROUP_WIDTH = SB_GROUP_HEADS * HEAD_DIM


def _suffix_sum_matrix(n_keys):
    j = np.arange(n_keys)[:, None]
    s = np.arange(n_keys)[None, :]
    u = (j > s).astype(np.float32)
    return jnp.asarray(np.concatenate([u, u], axis=0), BF16)


def _stickbreak_kernel(q_ref, k_ref, v_ref, u_ref, o_ref, c_ref, acc_ref):
    step = pl.program_id(1)
    n_sub = q_ref.shape[1] // BLOCK
    n_groups = q_ref.shape[2] // SB_GROUP_WIDTH
    group_rows = SB_GROUP_HEADS * BLOCK
    group_cols = [slice(g * SB_GROUP_WIDTH, (g + 1) * SB_GROUP_WIDTH) for g in range(n_groups)]
    head_of_lane = lax.broadcasted_iota(jnp.int32, (BLOCK, SB_GROUP_WIDTH), 1) // HEAD_DIM

    def stack_heads(q):
        zero = jnp.zeros_like(q)
        return jnp.concatenate([jnp.where(head_of_lane == h, q, zero) for h in range(SB_GROUP_HEADS)], axis=0)

    q_stacked = {(sub, g): stack_heads(q_ref[0, sub * BLOCK:(sub + 1) * BLOCK, group_cols[g]])
                 for sub in range(n_sub) for g in range(n_groups)}
    def head_rows(g, r0, nr):
        return [slice(g * group_rows + h * BLOCK + r0, g * group_rows + h * BLOCK + r0 + nr)
                for h in range(SB_GROUP_HEADS)]

    def windows(jobs):
        units = [job + (g,) for job in jobs for g in range(n_groups)]

        scores = []
        for sub, k0, n_blocks, mask_from, r0, nr, fresh, g in units:
            qs = q_stacked[sub, g]
            if nr != BLOCK:
                qs = jnp.concatenate([qs[h * BLOCK + r0:h * BLOCK + r0 + nr] for h in range(SB_GROUP_HEADS)], axis=0)
            scores.append(_dot_nt(qs, k_ref[0, pl.ds(k0, n_blocks * BLOCK), group_cols[g]]))
        stage_b = []
        for (sub, k0, n_blocks, mask_from, r0, nr, fresh, g), z_all in zip(units, scores):
            if mask_from < n_blocks:
                row_iota = lax.broadcasted_iota(jnp.int32, (nr, BLOCK), 0)
                q_pos = (step * n_sub + sub) * BLOCK + r0 + jnp.concatenate([row_iota] * SB_GROUP_HEADS, axis=0)
                key_in_block = lax.broadcasted_iota(jnp.int32, (SB_GROUP_HEADS * nr, BLOCK), 1)
            log_betas, split, later = {}, {}, {}
            run = None
            for j in reversed(range(n_blocks)):
                z = z_all[:, j * BLOCK:(j + 1) * BLOCK]
                cost = jnp.maximum(z, 0.0) + jnp.log(1.0 + jnp.exp(-jnp.abs(z)))
                log_beta = z - cost
                if j >= mask_from:
                    valid = (k0 + j * BLOCK + key_in_block) < q_pos
                    cost = jnp.where(valid, cost, 0.0)
                    log_beta = jnp.where(valid, log_beta, MASKED_SCORE)
                hi = cost.astype(BF16)
                lo = (cost - hi.astype(F32)).astype(BF16)
                log_betas[j] = log_beta
                split[j] = jnp.concatenate([hi, lo], axis=1)
                later[j] = run
                tot = jnp.sum(cost, axis=1, keepdims=True)
                run = tot if run is None else run + tot
            within = _dot(jnp.concatenate([split[j] for j in range(n_blocks)], axis=0), u_ref[...])
            stage_b.append((log_betas, later, run, within))
        for (sub, k0, n_blocks, mask_from, r0, nr, fresh, g), (log_betas, later, total, within) in zip(units, stage_b):
            unit_rows = SB_GROUP_HEADS * nr
            if fresh:
                c_old = None
            else:
                c_old = jnp.concatenate([c_ref[sub, rows, :] for rows in head_rows(g, r0, nr)], axis=0)
            weights = []
            for j in range(n_blocks):
                cost_after = within[j * unit_rows:(j + 1) * unit_rows]
                if later[j] is not None:
                    cost_after = cost_after + later[j]
                if c_old is not None:
                    cost_after = cost_after + c_old
                weights.append(jnp.exp(log_betas[j] - cost_after).astype(BF16))
            c_new = jnp.broadcast_to(total, (unit_rows, LANES)) if fresh else c_old + total
            for h, rows in enumerate(head_rows(g, r0, nr)):
                c_ref[sub, rows, :] = c_new[h * nr:(h + 1) * nr]
            pv = _dot(jnp.concatenate(weights, axis=1),
                      v_ref[0, pl.ds(k0, n_blocks * BLOCK), group_cols[g]])
            lane_head = lax.broadcasted_iota(jnp.int32, (nr, SB_GROUP_WIDTH), 1) // HEAD_DIM
            picked = pv[(SB_GROUP_HEADS - 1) * nr:]
            for h in reversed(range(SB_GROUP_HEADS - 1)):
                picked = jnp.where(lane_head == h, pv[h * nr:(h + 1) * nr], picked)
            if fresh:
                acc_ref[sub, r0:r0 + nr, group_cols[g]] = picked
            else:
                acc_ref[sub, r0:r0 + nr, group_cols[g]] += picked

    def all_dead(sub, r0=0, nr=BLOCK):
        least = None
        for g in range(n_groups):
            for sl in head_rows(g, r0, nr):
                least = c_ref[sub, sl, :] if least is None else jnp.minimum(least, c_ref[sub, sl, :])
        return jnp.min(least) > -F32_EXP_ZERO_BELOW

    qi0 = step * n_sub
    top = SB_TOP_ROWS

    near = SB_FIRST_BLOCKS - 1

    def key_start(block):
        return block * BLOCK if isinstance(block, int) else pl.multiple_of(block * BLOCK, BLOCK)

    def run_blocks(query_blocks):
        first_jobs, top_jobs, walkers = [], [], []
        for sub, qb in enumerate(query_blocks):
            if isinstance(qb, int) and qb < near:
                first_jobs.append((sub, 0, qb + 1, qb, 0, BLOCK, True))
            else:
                first_jobs.append((sub, key_start(qb - near + 1), near, near - 1, 0, BLOCK, True))
                top_jobs.append((sub, key_start(qb - near), 1, 1, 0, top, False))
                walkers.append((sub, qb - near))
        windows(first_jobs + top_jobs)
        top_dead = {sub: all_dead(sub, 0, top) for sub, _ in walkers}
        rest_dead = {sub: all_dead(sub, top, BLOCK - top) for sub, _ in walkers}
        for sub, far_block in walkers:
            @pl.when(jnp.logical_not(rest_dead[sub]))
            def _(sub=sub, far_block=far_block):
                windows([(sub, key_start(far_block), 1, 1, top, BLOCK - top, False)])

            if isinstance(far_block, int) and far_block == 0:
                continue

            def body(carry, sub=sub):
                kb, _ = carry
                windows([(sub, key_start(kb), 1, 1, 0, BLOCK, False)])
                done = jnp.logical_or(kb == 0, all_dead(sub))
                return kb - 1, done.astype(jnp.int32)

            done_first = jnp.logical_or(far_block == 0, jnp.logical_and(top_dead[sub], rest_dead[sub]))
            lax.while_loop(lambda c: c[1] == 0, body,
                           (jnp.asarray(far_block - 1, jnp.int32), done_first.astype(jnp.int32)))

    @pl.when(step > 0)
    def _():
        run_blocks([qi0 + sub for sub in range(n_sub)])

    @pl.when(step == 0)
    def _():
        run_blocks(list(range(n_sub)))

    for sub in range(n_sub):
        o_ref[0, sub * BLOCK:(sub + 1) * BLOCK, :] = acc_ref[sub].astype(o_ref.dtype)


SB_QUERY_BLOCKS_PER_STEP = 8


def _stickbreak(q, k, v):
    b, s, w = q.shape
    n_sub = SB_QUERY_BLOCKS_PER_STEP
    assert n_sub >= SB_FIRST_BLOCKS - 1 and s % (n_sub * BLOCK) == 0
    assert s >= SB_FIRST_BLOCKS * BLOCK and w % SB_GROUP_WIDTH == 0
    u = _suffix_sum_matrix(BLOCK)
    blk = pl.BlockSpec((1, n_sub * BLOCK, w), lambda bi, qi: (bi, qi, 0))
    full = pl.BlockSpec((1, s, w), lambda bi, qi: (bi, 0, 0))
    const = lambda arr: pl.BlockSpec(arr.shape, lambda bi, qi: (0, 0))
    n_heads = w // HEAD_DIM
    return pl.pallas_call(
        _stickbreak_kernel,
        grid=(b, s // (n_sub * BLOCK)),
        in_specs=[blk, full, full, const(u)],
        out_specs=blk,
        out_shape=jax.ShapeDtypeStruct((b, s, w), BF16),
        scratch_shapes=[pltpu.VMEM((n_sub, n_heads * BLOCK, LANES), F32), pltpu.VMEM((n_sub, BLOCK, w), F32)],
        compiler_params=_params("arbitrary", "arbitrary"),
        name="stickbreak",
    )(q, k, v, u)


MIX_ROWS = 1024
MIX_ROW_PARTS = 4


def _mix_xattn_kernel(o0, o1, o2, l0, l1, l2, sb, ga, gb, x_ref, wa, wb, wo, gx, wxq, kv, wxo, out_ref,
                      so1, sl1, so2, sl2):
    rows = x_ref.shape[0]

    def undilate(ref, stage_ref, dilation):
        n_planes = A_GROUP_WIDTH // LANES
        for r in range(dilation):
            for c in range(n_planes):
                c0 = r * A_GROUP_WIDTH + c * LANES
                stage_ref[c, pl.ds(r, rows // dilation, stride=dilation), :] = ref[:, c0:c0 + LANES].astype(F32)
        return jnp.concatenate([stage_ref[c] for c in range(n_planes)], axis=1)

    dilations = [d for _, d in DIL_GROUPS]
    assert dilations[0] == 1
    ob, lb = undilate(o1, so1, dilations[1]), undilate(l1, sl1, dilations[1])
    oc, lc = undilate(o2, so2, dilations[2]), undilate(l2, sl2, dilations[2])

    part_rows = rows // MIX_ROW_PARTS
    parts = [slice(p * part_rows, (p + 1) * part_rows) for p in range(MIX_ROW_PARTS)]

    def mixed_residual(r):
        yb = _dot(sb[r, :], wb[...])
        la = l0[r, :]
        m = jnp.maximum(jnp.maximum(la, lb[r]), lc[r])
        ea, eb, ec = jnp.exp(la - m), jnp.exp(lb[r] - m), jnp.exp(lc[r] - m)
        y = (ea * o0[r, :].astype(F32) + eb * ob[r] + ec * oc[r]) * (1.0 / (ea + eb + ec))
        ya = _dot(y.astype(BF16), wa[...])
        mix = ga[r, :].astype(F32) * ya + gb[r, :].astype(F32) * yb
        return x_ref[r, :] + _dot(mix.astype(BF16), wo[...])

    def cross_attention(q):
        scale = X_HEAD_DIM ** -0.5
        heads = []
        for hd in range(X_HEADS):
            cols = slice(hd * X_HEAD_DIM, (hd + 1) * X_HEAD_DIM)
            kh = kv[0, :, cols]
            vh = kv[0, :, X_WIDTH + hd * X_HEAD_DIM:X_WIDTH + (hd + 1) * X_HEAD_DIM]
            s = _dot_nt(q[:, cols].astype(BF16), kh) * scale
            mx = jnp.max(s, axis=1, keepdims=True)
            p = jnp.exp(s - mx)
            l = jnp.sum(p, axis=1, keepdims=True)
            heads.append((_dot(p.astype(BF16), vh) * (1.0 / l)).astype(BF16))
        return jnp.concatenate(heads, axis=1)

    hs = [mixed_residual(r) for r in parts]
    qs = [_dot(_rms_norm_bf16(h, gx[...]), wxq[...]) for h in hs]
    os_ = [cross_attention(q) for q in qs]
    for r, h, o in zip(parts, hs, os_):
        out_ref[r, :] = h + _dot(o, wxo[...])


def _mix_xattn(o_groups, lse_groups, sb, ga, gb, x2, wa, wb, wo, gx, wxq, kv, wxo, seq):
    t_rows, d = x2.shape
    tm = MIX_ROWS
    tiles_per_seq = seq // tm
    row = lambda width: pl.BlockSpec((tm, width), lambda i: (i, 0))
    dil = lambda dilation: pl.BlockSpec((tm // dilation, dilation * A_GROUP_WIDTH), lambda i: (i, 0))
    const = lambda arr: pl.BlockSpec(arr.shape, lambda i: (0,) * arr.ndim)
    ml, kvw = kv.shape[1], kv.shape[2]
    group_specs = [dil(dilation) for _, dilation in DIL_GROUPS]
    return pl.pallas_call(
        _mix_xattn_kernel,
        grid=(t_rows // tm,),
        in_specs=group_specs + group_specs + [row(SB_WIDTH), row(d), row(d), row(d),
                  const(wa), const(wb), const(wo), const(gx), const(wxq),
                  pl.BlockSpec((1, ml, kvw), lambda i: (i // tiles_per_seq, 0, 0)),
                  const(wxo)],
        out_specs=row(d),
        out_shape=jax.ShapeDtypeStruct((t_rows, d), F32),
        scratch_shapes=[pltpu.VMEM((A_GROUP_WIDTH // LANES, tm, LANES), F32)] * 4,
        compiler_params=_params("arbitrary"),
        name="mix_xattn",
    )(*o_groups, *lse_groups, sb, ga, gb, x2, wa, wb, wo, gx, wxq, kv, wxo)


FFN_ROWS = 512
FFN_CHUNK = 256
HALO = SUBLANES


def _ffn_kernel(h_ref, g_ref, wup_hbm, cw_ref, cb_ref, wdn_hbm, gf_ref, out_ref,
                halo_ref, u_ref, act_ref, wup, wdn, up_stage, dn_stage, sems, *, tiles_per_seq):
    i = pl.program_id(0)
    rows = h_ref.shape[0]
    d_ff = wdn.shape[0]

    @pl.when(i == 0)
    def _():
        up_rows = up_stage.shape[1]
        _fetch_as_bf16(wup_hbm, wup, up_stage, sems,
                       [(pl.ds(r * up_rows, up_rows), slice(None)) for r in range(wup.shape[0] // up_rows)])
        dn_rows = dn_stage.shape[1]
        _fetch_as_bf16(wdn_hbm, wdn, dn_stage, sems,
                       [(pl.ds(r * dn_rows, dn_rows), slice(None)) for r in range(d_ff // dn_rows)])

    @pl.when(i % tiles_per_seq == 0)
    def _():
        halo_ref[...] = jnp.zeros_like(halo_ref)

    h = h_ref[...]
    n = _rms_norm_bf16(h, g_ref[...])

    def conv(c0):
        cols = slice(c0, c0 + FFN_CHUNK)
        u = _dot(n, wup[:, cols])
        u_ref[0:HALO, :] = halo_ref[:, cols]
        u_ref[HALO:HALO + rows, :] = u
        halo_ref[:, cols] = u[rows - HALO:, :]
        out = cb_ref[:, cols] + cw_ref[CONV_WIDTH - 1:CONV_WIDTH, cols] * u
        for back in range(1, CONV_WIDTH):
            tap = CONV_WIDTH - 1 - back
            out = out + cw_ref[tap:tap + 1, cols] * u_ref[HALO - back:HALO - back + rows, :]
        return out

    for c in range(d_ff // FFN_CHUNK):
        gate = conv(c * FFN_CHUNK)
        val = conv(d_ff + c * FFN_CHUNK)
        act = gate * (1.0 / (1.0 + jnp.exp(-gate))) * val
        act_ref[:, c * FFN_CHUNK:(c + 1) * FFN_CHUNK] = act.astype(BF16)

    y = h + _dot(act_ref[...], wdn[...])
    ms = jnp.mean(y * y, axis=-1, keepdims=True)
    out_ref[...] = y * lax.rsqrt(ms + EPS) * gf_ref[...]


FFN_UP_WEIGHT_CHUNK = 128
FFN_DOWN_WEIGHT_CHUNK = 256


def _ffn(h2, g, wup, conv_w, conv_b, wdn, gf, seq):
    t_rows, d = h2.shape
    tm = FFN_ROWS
    d_ff = wdn.shape[0]
    assert d_ff % FFN_CHUNK == 0 and seq % tm == 0
    assert wup.shape[0] % FFN_UP_WEIGHT_CHUNK == 0 and d_ff % FFN_DOWN_WEIGHT_CHUNK == 0
    const = lambda arr: pl.BlockSpec(arr.shape, lambda i: (0,) * arr.ndim)
    in_hbm = pl.BlockSpec(memory_space=pl.ANY)
    return pl.pallas_call(
        functools.partial(_ffn_kernel, tiles_per_seq=seq // tm),
        grid=(t_rows // tm,),
        in_specs=[pl.BlockSpec((tm, d), lambda i: (i, 0)),
                  const(g), in_hbm, const(conv_w), const(conv_b), in_hbm, const(gf)],
        out_specs=pl.BlockSpec((tm, d), lambda i: (i, 0)),
        out_shape=jax.ShapeDtypeStruct((t_rows, d), F32),
        scratch_shapes=[pltpu.VMEM((HALO, 2 * d_ff), F32),
                        pltpu.VMEM((HALO + tm, FFN_CHUNK), F32),
                        pltpu.VMEM((tm, d_ff), BF16),
                        pltpu.VMEM(wup.shape, BF16),
                        pltpu.VMEM(wdn.shape, BF16),
                        pltpu.VMEM((2, FFN_UP_WEIGHT_CHUNK, wup.shape[1]), F32),
                        pltpu.VMEM((2, FFN_DOWN_WEIGHT_CHUNK, wdn.shape[1]), F32),
                        pltpu.SemaphoreType.DMA((2,))],
        compiler_params=_params("arbitrary"),
        name="ffn",
    )(h2, g, wup, conv_w, conv_b, wdn, gf)


def _rope_tables(seq):
    half = HEAD_DIM // 2
    f32 = np.float32
    inv_freq = f32(1.0) / (f32(ROPE_THETA) ** (np.arange(half, dtype=f32) * f32(2.0 / HEAD_DIM)))
    ang = np.arange(seq, dtype=f32)[:, None] * inv_freq[None, :]
    cos, sin = np.cos(ang), np.sin(ang)
    reps = LANES // HEAD_DIM
    cos_t = np.tile(np.concatenate([cos, cos], axis=1), (1, reps))
    sin_t = np.tile(np.concatenate([-sin, sin], axis=1), (1, reps))
    return jnp.asarray(cos_t, F32), jnp.asarray(sin_t, F32)


def kernel(x, mem, ln_mix_g, w_in, b_gate, w_branch_a, w_branch_b, w_out, ln_x_g, ln_mem_g, w_xq, w_xkv, w_xo,
           ln_ffn_g, w_up, conv_w, conv_b, w_down, ln_f_g):
    b, s, d = x.shape
    depth = w_in.shape[0]
    assert depth == 1, "the final norm is fused into the (single) layer's ffn kernel"
    cos_t, sin_t = _rope_tables(s)
    h = x.reshape(b * s, d)
    for l in range(depth):
        (qa0, qa1, qa2, ka0, ka1, ka2, va0, va1, va2, qb, kb, vb, ga, gb) = _in_proj(
            h, ln_mix_g[l][None], w_in[l], b_gate[l][None], cos_t, sin_t, s)
        o_groups, lse_groups = [], []
        for g, ((window, dilation), q, k, v) in enumerate(zip(
                DIL_GROUPS, (qa0, qa1, qa2), (ka0, ka1, ka2), (va0, va1, va2))):
            o, lse = _dil_attn(q, k, v, b, window // dilation, f"dil_attn_g{g}")
            o_groups.append(o)
            lse_groups.append(lse)
        sb3 = (b, s, SB_WIDTH)
        sb = _stickbreak(qb.reshape(sb3), kb.reshape(sb3), vb.reshape(sb3)).reshape(b * s, SB_WIDTH)
        kv = _mem_kv(mem, ln_mem_g[l][None], w_xkv[l].astype(BF16))
        h = _mix_xattn(o_groups, lse_groups, sb, ga, gb, h,
                       w_branch_a[l].astype(BF16), w_branch_b[l].astype(BF16), w_out[l].astype(BF16),
                       ln_x_g[l][None], w_xq[l].astype(BF16), kv, w_xo[l].astype(BF16), s)
        h = _ffn(h, ln_ffn_g[l][None], w_up[l], conv_w[l], conv_b[l][None], w_down[l], ln_f_g[None], s)
    return h.reshape(b, s, d)
```

```python
import functools

import jax
import jax.numpy as jnp
import numpy as np
from jax import lax
from jax.experimental import pallas as pl
from jax.experimental.pallas import tpu as pltpu

F32 = jnp.float32
BF16 = jnp.bfloat16

LANES = 128
SUBLANES = 8
VMEM_LIMIT_BYTES = 56 * 1024 * 1024

HEAD_DIM = 64
DIL_GROUPS = ((128, 1), (512, 4), (2048, 16))
A_HEADS_PER_GROUP = 4
A_GROUP_WIDTH = A_HEADS_PER_GROUP * HEAD_DIM
SB_HEADS = 8
SB_WIDTH = SB_HEADS * HEAD_DIM
BLOCK = 128
ROPE_THETA = 10000.0
X_HEADS = 4
X_HEAD_DIM = 128
X_WIDTH = X_HEADS * X_HEAD_DIM
CONV_WIDTH = 3
EPS = 1e-6

F32_EXP_ZERO_BELOW = -104.0
MASKED_SCORE = -1e30


def _rms_norm_bf16(x, g):
    ms = jnp.mean(x * x, axis=-1, keepdims=True)
    return (x * lax.rsqrt(ms + EPS) * g).astype(BF16)


def _dot(a, b):
    return jnp.dot(a, b, preferred_element_type=F32)


def _dot_nt(a, b):
    return lax.dot_general(a, b, (((1,), (1,)), ((), ())), preferred_element_type=F32)


def _params(*semantics):
    return pltpu.CompilerParams(dimension_semantics=semantics, vmem_limit_bytes=VMEM_LIMIT_BYTES)


def _fetch_as_bf16(src_hbm, dst_ref, stage_ref, sems, chunks):
    copies = [pltpu.make_async_copy(src_hbm.at[idx], stage_ref.at[k % 2], sems.at[k % 2])
              for k, idx in enumerate(chunks)]
    for copy in copies[:2]:
        copy.start()
    for k, idx in enumerate(chunks):
        copies[k].wait()
        dst_ref[idx] = stage_ref[k % 2].astype(BF16)
        if k + 2 < len(chunks):
            copies[k + 2].start()


def _mem_kv_kernel(m_ref, g_ref, w_ref, o_ref):
    n = _rms_norm_bf16(m_ref[0], g_ref[...])
    o_ref[0] = _dot(n, w_ref[...]).astype(BF16)


def _mem_kv(mem, g, w_bf16):
    b, ml, d = mem.shape
    n_out = w_bf16.shape[1]
    return pl.pallas_call(
        _mem_kv_kernel,
        grid=(b,),
        in_specs=[pl.BlockSpec((1, ml, d), lambda i: (i, 0, 0)),
                  pl.BlockSpec((1, d), lambda i: (0, 0)),
                  pl.BlockSpec((d, n_out), lambda i: (0, 0))],
        out_specs=pl.BlockSpec((1, ml, n_out), lambda i: (i, 0, 0)),
        out_shape=jax.ShapeDtypeStruct((b, ml, n_out), BF16),
        compiler_params=_params("arbitrary"),
        name="mem_kv",
    )(mem, g, w_bf16)


IN_PROJ_ROWS = 1024


def _in_proj_kernel(x_ref, g_ref, w_hbm, bg_ref, cos_ref, sin_ref,
                    qa0, qa1, qa2, ka0, ka1, ka2, va0, va1, va2, qb, kb, vb, ga, gb,
                    stage_ref, w_ref, w_stage, sems):
    @pl.when(pl.program_id(0) == 0)
    def _():
        w_rows = w_stage.shape[1]
        _fetch_as_bf16(w_hbm, w_ref, w_stage, sems,
                       [(pl.ds(r * w_rows, w_rows), slice(None)) for r in range(w_ref.shape[0] // w_rows)])

    n = _rms_norm_bf16(x_ref[...], g_ref[...])
    rows = n.shape[0]
    cos = cos_ref[...]
    sin = sin_ref[...]
    lane = lax.broadcasted_iota(jnp.int32, (rows, LANES), 1)
    first_half = (lane % HEAD_DIM) < (HEAD_DIM // 2)

    def rope(t):
        partner = jnp.where(first_half, pltpu.roll(t, LANES - HEAD_DIM // 2, 1),
                            pltpu.roll(t, HEAD_DIM // 2, 1))
        return t * cos + partner * sin

    def proj(c0, width):
        return _dot(n, w_ref[:, c0:c0 + width])

    def store_dilated(ref, t, dilation):
        if dilation == 1:
            ref[...] = t.astype(BF16)
            return
        n_planes = A_GROUP_WIDTH // LANES
        for c in range(n_planes):
            stage_ref[c] = t[:, c * LANES:(c + 1) * LANES]
        for r in range(dilation):
            for c in range(n_planes):
                c0 = r * A_GROUP_WIDTH + c * LANES
                ref[:, c0:c0 + LANES] = stage_ref[c, pl.ds(r, rows // dilation, stride=dilation), :].astype(BF16)

    q_scale = HEAD_DIM ** -0.5
    dilations = [d for _, d in DIL_GROUPS]
    chunk = 256
    d_model = ga.shape[1]
    a_cols = 3 * len(DIL_GROUPS) * A_GROUP_WIDTH
    sb_cols = 3 * SB_WIDTH

    def window_attention_columns():
        col = 0
        for refs, roped, scale in (((qa0, qa1, qa2), True, q_scale), ((ka0, ka1, ka2), True, None),
                                   ((va0, va1, va2), False, None)):
            for ref, dilation in zip(refs, dilations):
                t = proj(col, A_GROUP_WIDTH)
                if roped:
                    t = jnp.concatenate([rope(t[:, h * LANES:(h + 1) * LANES])
                                         for h in range(A_GROUP_WIDTH // LANES)], axis=1)
                if scale is not None:
                    t = t * scale
                store_dilated(ref, t, dilation)
                col += A_GROUP_WIDTH

    def stickbreak_columns():
        col = a_cols
        for ref, scale in ((qb, q_scale), (kb, None), (vb, None)):
            for c in range(SB_WIDTH // chunk):
                t = proj(col, chunk)
                if scale is not None:
                    t = t * scale
                ref[:, c * chunk:(c + 1) * chunk] = t.astype(BF16)
                col += chunk

    def gate_columns():
        col, gcol = a_cols + sb_cols, 0
        for ref in (ga, gb):
            for c in range(d_model // chunk):
                t = proj(col, chunk) + bg_ref[:, gcol:gcol + chunk]
                ref[:, c * chunk:(c + 1) * chunk] = (1.0 / (1.0 + jnp.exp(-t))).astype(BF16)
                col += chunk
                gcol += chunk

    gate_columns()
    window_attention_columns()
    stickbreak_columns()


IN_PROJ_WEIGHT_CHUNK = 64


def _in_proj(x2, g, w_in, b_gate, cos_t, sin_t, seq):
    t_rows, d = x2.shape
    tm = IN_PROJ_ROWS
    n_cols = w_in.shape[1]
    assert d % IN_PROJ_WEIGHT_CHUNK == 0
    tiles_per_seq = seq // tm
    dilations = [dil for _, dil in DIL_GROUPS]
    outs = [(dil, dil * A_GROUP_WIDTH) for dil in dilations] * 3 + [(1, SB_WIDTH)] * 3 + [(1, d)] * 2
    return pl.pallas_call(
        _in_proj_kernel,
        grid=(t_rows // tm,),
        in_specs=[pl.BlockSpec((tm, d), lambda i: (i, 0)),
                  pl.BlockSpec((1, d), lambda i: (0, 0)),
                  pl.BlockSpec(memory_space=pl.ANY),
                  pl.BlockSpec((1, 2 * d), lambda i: (0, 0)),
                  pl.BlockSpec((tm, LANES), lambda i: (i % tiles_per_seq, 0)),
                  pl.BlockSpec((tm, LANES), lambda i: (i % tiles_per_seq, 0))],
        out_specs=[pl.BlockSpec((tm // div, w), lambda i: (i, 0)) for div, w in outs],
        out_shape=[jax.ShapeDtypeStruct((t_rows // div, w), BF16) for div, w in outs],
        scratch_shapes=[pltpu.VMEM((A_GROUP_WIDTH // LANES, tm, LANES), F32),
                        pltpu.VMEM((d, n_cols), BF16),
                        pltpu.VMEM((2, IN_PROJ_WEIGHT_CHUNK, n_cols), F32),
                        pltpu.SemaphoreType.DMA((2,))],
        compiler_params=_params("arbitrary"),
        name="in_proj",
    )(x2, g, w_in, b_gate, cos_t, sin_t)


def _dil_attn_kernel(q_ref, kc_ref, kp_ref, vc_ref, vp_ref, o_ref, lse_ref, *, n_back):
    j = pl.program_id(2)
    rows, width = q_ref.shape[1], q_ref.shape[2]
    row = lax.broadcasted_iota(jnp.int32, (2 * BLOCK, 2 * BLOCK), 0) % BLOCK
    col = lax.broadcasted_iota(jnp.int32, (2 * BLOCK, 2 * BLOCK), 1)
    dist = row - col + BLOCK
    band = (dist >= 0) & (dist <= n_back)
    first_col = jnp.where(j > 0, 0, BLOCK)
    band_first = band & (col >= first_col)
    lane = lax.broadcasted_iota(jnp.int32, (BLOCK, LANES), 1)
    head0 = lane < HEAD_DIM

    for cb in range(width // LANES):
        lanes = slice(cb * LANES, (cb + 1) * LANES)
        for sub in range(rows // BLOCK):
            cur = slice(sub * BLOCK, (sub + 1) * BLOCK)
            q = q_ref[0, cur, lanes]
            if sub == 0:
                k2 = jnp.concatenate([kp_ref[0, :, lanes], kc_ref[0, cur, lanes]], axis=0)
                v2 = jnp.concatenate([vp_ref[0, :, lanes], vc_ref[0, cur, lanes]], axis=0)
                mask = band_first
            else:
                both = slice((sub - 1) * BLOCK, (sub + 1) * BLOCK)
                k2 = kc_ref[0, both, lanes]
                v2 = vc_ref[0, both, lanes]
                mask = band
            zero = jnp.zeros_like(q)
            q2 = jnp.concatenate([jnp.where(head0, q, zero), jnp.where(head0, zero, q)], axis=0)
            s = jnp.where(mask, _dot_nt(q2, k2), MASKED_SCORE)
            m = jnp.max(s, axis=1, keepdims=True)
            p = jnp.exp(s - m)
            l = jnp.sum(p, axis=1, keepdims=True)
            o2 = _dot(p.astype(BF16), v2) * (1.0 / l)
            lse2 = m + jnp.log(l)
            o_ref[0, cur, lanes] = jnp.where(head0, o2[:BLOCK], o2[BLOCK:]).astype(o_ref.dtype)
            lse_ref[0, cur, lanes] = jnp.where(head0, lse2[:BLOCK], lse2[BLOCK:])


DIL_ATTN_BLOCKS_PER_STEP = 64


def _dil_attn(q, k, v, batch, n_back, name):
    assert n_back <= BLOCK
    total_rows, width = q.shape
    length = total_rows // batch
    chunk_width = min(width, 4 * LANES)
    rows = min(length, BLOCK * DIL_ATTN_BLOCKS_PER_STEP * LANES // chunk_width)
    blocks_per_chunk = rows // BLOCK
    chunk_width = min(width, LANES * DIL_ATTN_BLOCKS_PER_STEP // blocks_per_chunk)
    view = lambda t: t.reshape(batch, length, width)
    cur_spec = pl.BlockSpec((1, rows, chunk_width), lambda bi, c, j: (bi, j, c))
    prev_spec = pl.BlockSpec((1, BLOCK, chunk_width),
                             lambda bi, c, j: (bi, jnp.maximum(j * blocks_per_chunk - 1, 0), c))
    o, lse = pl.pallas_call(
        functools.partial(_dil_attn_kernel, n_back=n_back),
        grid=(batch, width // chunk_width, length // rows),
        in_specs=[cur_spec, cur_spec, prev_spec, cur_spec, prev_spec],
        out_specs=[cur_spec, cur_spec],
        out_shape=[jax.ShapeDtypeStruct((batch, length, width), BF16),
                   jax.ShapeDtypeStruct((batch, length, width), F32)],
        compiler_params=_params("arbitrary", "arbitrary", "arbitrary"),
        name=name,
    )(view(q), view(k), view(k), view(v), view(v))
    return o.reshape(total_rows, width), lse.reshape(total_rows, width)


SB_FIRST_BLOCKS = 3
SB_TOP_ROWS = 32
SB_GROUP_HEADS = 4
SB_GROUP_WIDTH = SB_GROUP_HEADS * HEAD_DIM


def _suffix_sum_matrix(n_keys):
    j = np.arange(n_keys)[:, None]
    s = np.arange(n_keys)[None, :]
    u = (j > s).astype(np.float32)
    return jnp.asarray(np.concatenate([u, u], axis=0), BF16)


def _stickbreak_kernel(q_ref, k_ref, v_ref, u_ref, o_ref, c_ref, acc_ref):
    step = pl.program_id(1)
    n_sub = q_ref.shape[1] // BLOCK
    n_groups = q_ref.shape[2] // SB_GROUP_WIDTH
    group_rows = SB_GROUP_HEADS * BLOCK
    group_cols = [slice(g * SB_GROUP_WIDTH, (g + 1) * SB_GROUP_WIDTH) for g in range(n_groups)]
    head_of_lane = lax.broadcasted_iota(jnp.int32, (BLOCK, SB_GROUP_WIDTH), 1) // HEAD_DIM

    def stack_heads(q):
        zero = jnp.zeros_like(q)
        return jnp.concatenate([jnp.where(head_of_lane == h, q, zero) for h in range(SB_GROUP_HEADS)], axis=0)

    q_stacked = {(sub, g): stack_heads(q_ref[0, sub * BLOCK:(sub + 1) * BLOCK, group_cols[g]])
                 for sub in range(n_sub) for g in range(n_groups)}
    def head_rows(g, r0, nr):
        return [slice(g * group_rows + h * BLOCK + r0, g * group_rows + h * BLOCK + r0 + nr)
                for h in range(SB_GROUP_HEADS)]

    def windows(jobs):
        units = [job + (g,) for job in jobs for g in range(n_groups)]

        scores = []
        for sub, k0, n_blocks, mask_from, r0, nr, fresh, g in units:
            qs = q_stacked[sub, g]
            if nr != BLOCK:
                qs = jnp.concatenate([qs[h * BLOCK + r0:h * BLOCK + r0 + nr] for h in range(SB_GROUP_HEADS)], axis=0)
            scores.append(_dot_nt(qs, k_ref[0, pl.ds(k0, n_blocks * BLOCK), group_cols[g]]))
        stage_b = []
        for (sub, k0, n_blocks, mask_from, r0, nr, fresh, g), z_all in zip(units, scores):
            if mask_from < n_blocks:
                row_iota = lax.broadcasted_iota(jnp.int32, (nr, BLOCK), 0)
                q_pos = (step * n_sub + sub) * BLOCK + r0 + jnp.concatenate([row_iota] * SB_GROUP_HEADS, axis=0)
                key_in_block = lax.broadcasted_iota(jnp.int32, (SB_GROUP_HEADS * nr, BLOCK), 1)
            log_betas, split, later = {}, {}, {}
            run = None
            for j in reversed(range(n_blocks)):
                z = z_all[:, j * BLOCK:(j + 1) * BLOCK]
                cost = jnp.maximum(z, 0.0) + jnp.log(1.0 + jnp.exp(-jnp.abs(z)))
                log_beta = z - cost
                if j >= mask_from:
                    valid = (k0 + j * BLOCK + key_in_block) < q_pos
                    cost = jnp.where(valid, cost, 0.0)
                    log_beta = jnp.where(valid, log_beta, MASKED_SCORE)
                hi = cost.astype(BF16)
                lo = (cost - hi.astype(F32)).astype(BF16)
                log_betas[j] = log_beta
                split[j] = jnp.concatenate([hi, lo], axis=1)
                later[j] = run
                tot = jnp.sum(cost, axis=1, keepdims=True)
                run = tot if run is None else run + tot
            within = _dot(jnp.concatenate([split[j] for j in range(n_blocks)], axis=0), u_ref[...])
            stage_b.append((log_betas, later, run, within))
        for (sub, k0, n_blocks, mask_from, r0, nr, fresh, g), (log_betas, later, total, within) in zip(units, stage_b):
            unit_rows = SB_GROUP_HEADS * nr
            if fresh:
                c_old = None
            else:
                c_old = jnp.concatenate([c_ref[sub, rows, :] for rows in head_rows(g, r0, nr)], axis=0)
            weights = []
            for j in range(n_blocks):
                cost_after = within[j * unit_rows:(j + 1) * unit_rows]
                if later[j] is not None:
                    cost_after = cost_after + later[j]
                if c_old is not None:
                    cost_after = cost_after + c_old
                weights.append(jnp.exp(log_betas[j] - cost_after).astype(BF16))
            c_new = jnp.broadcast_to(total, (unit_rows, LANES)) if fresh else c_old + total
            for h, rows in enumerate(head_rows(g, r0, nr)):
                c_ref[sub, rows, :] = c_new[h * nr:(h + 1) * nr]
            pv = _dot(jnp.concatenate(weights, axis=1),
                      v_ref[0, pl.ds(k0, n_blocks * BLOCK), group_cols[g]])
            lane_head = lax.broadcasted_iota(jnp.int32, (nr, SB_GROUP_WIDTH), 1) // HEAD_DIM
            picked = pv[(SB_GROUP_HEADS - 1) * nr:]
            for h in reversed(range(SB_GROUP_HEADS - 1)):
                picked = jnp.where(lane_head == h, pv[h * nr:(h + 1) * nr], picked)
            if fresh:
                acc_ref[sub, r0:r0 + nr, group_cols[g]] = picked
            else:
                acc_ref[sub, r0:r0 + nr, group_cols[g]] += picked

    def all_dead(sub, r0=0, nr=BLOCK):
        least = None
        for g in range(n_groups):
            for sl in head_rows(g, r0, nr):
                least = c_ref[sub, sl, :] if least is None else jnp.minimum(least, c_ref[sub, sl, :])
        return jnp.min(least) > -F32_EXP_ZERO_BELOW

    qi0 = step * n_sub
    top = SB_TOP_ROWS

    near = SB_FIRST_BLOCKS - 1

    def key_start(block):
        return block * BLOCK if isinstance(block, int) else pl.multiple_of(block * BLOCK, BLOCK)

    def run_blocks(query_blocks):
        first_jobs, top_jobs, walkers = [], [], []
        for sub, qb in enumerate(query_blocks):
            if isinstance(qb, int) and qb < near:
                first_jobs.append((sub, 0, qb + 1, qb, 0, BLOCK, True))
            else:
                first_jobs.append((sub, key_start(qb - near + 1), near, near - 1, 0, BLOCK, True))
                top_jobs.append((sub, key_start(qb - near), 1, 1, 0, top, False))
                walkers.append((sub, qb - near))
        windows(first_jobs + top_jobs)
        top_dead = {sub: all_dead(sub, 0, top) for sub, _ in walkers}
        rest_dead = {sub: all_dead(sub, top, BLOCK - top) for sub, _ in walkers}
        for sub, far_block in walkers:
            @pl.when(jnp.logical_not(rest_dead[sub]))
            def _(sub=sub, far_block=far_block):
                windows([(sub, key_start(far_block), 1, 1, top, BLOCK - top, False)])

            if isinstance(far_block, int) and far_block == 0:
                continue

            def body(carry, sub=sub):
                kb, _ = carry
                windows([(sub, key_start(kb), 1, 1, 0, BLOCK, False)])
                done = jnp.logical_or(kb == 0, all_dead(sub))
                return kb - 1, done.astype(jnp.int32)

            done_first = jnp.logical_or(far_block == 0, jnp.logical_and(top_dead[sub], rest_dead[sub]))
            lax.while_loop(lambda c: c[1] == 0, body,
                           (jnp.asarray(far_block - 1, jnp.int32), done_first.astype(jnp.int32)))

    @pl.when(step > 0)
    def _():
        run_blocks([qi0 + sub for sub in range(n_sub)])

    @pl.when(step == 0)
    def _():
        run_blocks(list(range(n_sub)))

    for sub in range(n_sub):
        o_ref[0, sub * BLOCK:(sub + 1) * BLOCK, :] = acc_ref[sub].astype(o_ref.dtype)


SB_QUERY_BLOCKS_PER_STEP = 4


def _stickbreak(q, k, v):
    b, s, w = q.shape
    n_sub = SB_QUERY_BLOCKS_PER_STEP
    assert n_sub >= SB_FIRST_BLOCKS - 1 and s % (n_sub * BLOCK) == 0
    assert s >= SB_FIRST_BLOCKS * BLOCK and w % SB_GROUP_WIDTH == 0
    u = _suffix_sum_matrix(BLOCK)
    blk = pl.BlockSpec((1, n_sub * BLOCK, w), lambda bi, qi: (bi, qi, 0))
    full = pl.BlockSpec((1, s, w), lambda bi, qi: (bi, 0, 0))
    const = lambda arr: pl.BlockSpec(arr.shape, lambda bi, qi: (0, 0))
    n_heads = w // HEAD_DIM
    return pl.pallas_call(
        _stickbreak_kernel,
        grid=(b, s // (n_sub * BLOCK)),
        in_specs=[blk, full, full, const(u)],
        out_specs=blk,
        out_shape=jax.ShapeDtypeStruct((b, s, w), BF16),
        scratch_shapes=[pltpu.VMEM((n_sub, n_heads * BLOCK, LANES), F32), pltpu.VMEM((n_sub, BLOCK, w), F32)],
        compiler_params=_params("arbitrary", "arbitrary"),
        name="stickbreak",
    )(q, k, v, u)


MIX_ROWS = 1024
MIX_ROW_PARTS = 4


def _mix_xattn_kernel(o0, o1, o2, l0, l1, l2, sb, ga, gb, x_ref, wa, wb, wo, gx, wxq, kv, wxo, out_ref,
                      so1, sl1, so2, sl2):
    rows = x_ref.shape[0]

    def undilate(ref, stage_ref, dilation):
        n_planes = A_GROUP_WIDTH // LANES
        for r in range(dilation):
            for c in range(n_planes):
                c0 = r * A_GROUP_WIDTH + c * LANES
                stage_ref[c, pl.ds(r, rows // dilation, stride=dilation), :] = ref[:, c0:c0 + LANES].astype(F32)
        return jnp.concatenate([stage_ref[c] for c in range(n_planes)], axis=1)

    dilations = [d for _, d in DIL_GROUPS]
    assert dilations[0] == 1
    ob, lb = undilate(o1, so1, dilations[1]), undilate(l1, sl1, dilations[1])
    oc, lc = undilate(o2, so2, dilations[2]), undilate(l2, sl2, dilations[2])

    part_rows = rows // MIX_ROW_PARTS
    parts = [slice(p * part_rows, (p + 1) * part_rows) for p in range(MIX_ROW_PARTS)]

    def mixed_residual(r):
        yb = _dot(sb[r, :], wb[...])
        la = l0[r, :]
        m = jnp.maximum(jnp.maximum(la, lb[r]), lc[r])
        ea, eb, ec = jnp.exp(la - m), jnp.exp(lb[r] - m), jnp.exp(lc[r] - m)
        y = (ea * o0[r, :].astype(F32) + eb * ob[r] + ec * oc[r]) * (1.0 / (ea + eb + ec))
        ya = _dot(y.astype(BF16), wa[...])
        mix = ga[r, :].astype(F32) * ya + gb[r, :].astype(F32) * yb
        return x_ref[r, :] + _dot(mix.astype(BF16), wo[...])

    def cross_attention(q):
        scale = X_HEAD_DIM ** -0.5
        heads = []
        for hd in range(X_HEADS):
            cols = slice(hd * X_HEAD_DIM, (hd + 1) * X_HEAD_DIM)
            kh = kv[0, :, cols]
            vh = kv[0, :, X_WIDTH + hd * X_HEAD_DIM:X_WIDTH + (hd + 1) * X_HEAD_DIM]
            s = _dot_nt(q[:, cols].astype(BF16), kh) * scale
            mx = jnp.max(s, axis=1, keepdims=True)
            p = jnp.exp(s - mx)
            l = jnp.sum(p, axis=1, keepdims=True)
            heads.append((_dot(p.astype(BF16), vh) * (1.0 / l)).astype(BF16))
        return jnp.concatenate(heads, axis=1)

    hs = [mixed_residual(r) for r in parts]
    qs = [_dot(_rms_norm_bf16(h, gx[...]), wxq[...]) for h in hs]
    os_ = [cross_attention(q) for q in qs]
    for r, h, o in zip(parts, hs, os_):
        out_ref[r, :] = h + _dot(o, wxo[...])


def _mix_xattn(o_groups, lse_groups, sb, ga, gb, x2, wa, wb, wo, gx, wxq, kv, wxo, seq):
    t_rows, d = x2.shape
    tm = MIX_ROWS
    tiles_per_seq = seq // tm
    row = lambda width: pl.BlockSpec((tm, width), lambda i: (i, 0))
    dil = lambda dilation: pl.BlockSpec((tm // dilation, dilation * A_GROUP_WIDTH), lambda i: (i, 0))
    const = lambda arr: pl.BlockSpec(arr.shape, lambda i: (0,) * arr.ndim)
    ml, kvw = kv.shape[1], kv.shape[2]
    group_specs = [dil(dilation) for _, dilation in DIL_GROUPS]
    return pl.pallas_call(
        _mix_xattn_kernel,
        grid=(t_rows // tm,),
        in_specs=group_specs + group_specs + [row(SB_WIDTH), row(d), row(d), row(d),
                  const(wa), const(wb), const(wo), const(gx), const(wxq),
                  pl.BlockSpec((1, ml, kvw), lambda i: (i // tiles_per_seq, 0, 0)),
                  const(wxo)],
        out_specs=row(d),
        out_shape=jax.ShapeDtypeStruct((t_rows, d), F32),
        scratch_shapes=[pltpu.VMEM((A_GROUP_WIDTH // LANES, tm, LANES), F32)] * 4,
        compiler_params=_params("arbitrary"),
        name="mix_xattn",
    )(*o_groups, *lse_groups, sb, ga, gb, x2, wa, wb, wo, gx, wxq, kv, wxo)


FFN_ROWS = 512
FFN_CHUNK = 256
HALO = SUBLANES


def _ffn_kernel(h_ref, g_ref, wup_hbm, cw_ref, cb_ref, wdn_hbm, gf_ref, out_ref,
                halo_ref, u_ref, act_ref, wup, wdn, up_stage, dn_stage, sems, *, tiles_per_seq):
    i = pl.program_id(0)
    rows = h_ref.shape[0]
    d_ff = wdn.shape[0]

    @pl.when(i == 0)
    def _():
        up_rows = up_stage.shape[1]
        _fetch_as_bf16(wup_hbm, wup, up_stage, sems,
                       [(pl.ds(r * up_rows, up_rows), slice(None)) for r in range(wup.shape[0] // up_rows)])
        dn_rows = dn_stage.shape[1]
        _fetch_as_bf16(wdn_hbm, wdn, dn_stage, sems,
                       [(pl.ds(r * dn_rows, dn_rows), slice(None)) for r in range(d_ff // dn_rows)])

    @pl.when(i % tiles_per_seq == 0)
    def _():
        halo_ref[...] = jnp.zeros_like(halo_ref)

    h = h_ref[...]
    n = _rms_norm_bf16(h, g_ref[...])

    def conv(c0):
        cols = slice(c0, c0 + FFN_CHUNK)
        u = _dot(n, wup[:, cols])
        u_ref[0:HALO, :] = halo_ref[:, cols]
        u_ref[HALO:HALO + rows, :] = u
        halo_ref[:, cols] = u[rows - HALO:, :]
        out = cb_ref[:, cols] + cw_ref[CONV_WIDTH - 1:CONV_WIDTH, cols] * u
        for back in range(1, CONV_WIDTH):
            tap = CONV_WIDTH - 1 - back
            out = out + cw_ref[tap:tap + 1, cols] * u_ref[HALO - back:HALO - back + rows, :]
        return out

    for c in range(d_ff // FFN_CHUNK):
        gate = conv(c * FFN_CHUNK)
        val = conv(d_ff + c * FFN_CHUNK)
        act = gate * (1.0 / (1.0 + jnp.exp(-gate))) * val
        act_ref[:, c * FFN_CHUNK:(c + 1) * FFN_CHUNK] = act.astype(BF16)

    y = h + _dot(act_ref[...], wdn[...])
    ms = jnp.mean(y * y, axis=-1, keepdims=True)
    out_ref[...] = y * lax.rsqrt(ms + EPS) * gf_ref[...]


FFN_UP_WEIGHT_CHUNK = 128
FFN_DOWN_WEIGHT_CHUNK = 256


def _ffn(h2, g, wup, conv_w, conv_b, wdn, gf, seq):
    t_rows, d = h2.shape
    tm = FFN_ROWS
    d_ff = wdn.shape[0]
    assert d_ff % FFN_CHUNK == 0 and seq % tm == 0
    assert wup.shape[0] % FFN_UP_WEIGHT_CHUNK == 0 and d_ff % FFN_DOWN_WEIGHT_CHUNK == 0
    const = lambda arr: pl.BlockSpec(arr.shape, lambda i: (0,) * arr.ndim)
    in_hbm = pl.BlockSpec(memory_space=pl.ANY)
    return pl.pallas_call(
        functools.partial(_ffn_kernel, tiles_per_seq=seq // tm),
        grid=(t_rows // tm,),
        in_specs=[pl.BlockSpec((tm, d), lambda i: (i, 0)),
                  const(g), in_hbm, const(conv_w), const(conv_b), in_hbm, const(gf)],
        out_specs=pl.BlockSpec((tm, d), lambda i: (i, 0)),
        out_shape=jax.ShapeDtypeStruct((t_rows, d), F32),
        scratch_shapes=[pltpu.VMEM((HALO, 2 * d_ff), F32),
                        pltpu.VMEM((HALO + tm, FFN_CHUNK), F32),
                        pltpu.VMEM((tm, d_ff), BF16),
                        pltpu.VMEM(wup.shape, BF16),
                        pltpu.VMEM(wdn.shape, BF16),
                        pltpu.VMEM((2, FFN_UP_WEIGHT_CHUNK, wup.shape[1]), F32),
                        pltpu.VMEM((2, FFN_DOWN_WEIGHT_CHUNK, wdn.shape[1]), F32),
                        pltpu.SemaphoreType.DMA((2,))],
        compiler_params=_params("arbitrary"),
        name="ffn",
    )(h2, g, wup, conv_w, conv_b, wdn, gf)


def _rope_tables(seq):
    half = HEAD_DIM // 2
    f32 = np.float32
    inv_freq = f32(1.0) / (f32(ROPE_THETA) ** (np.arange(half, dtype=f32) * f32(2.0 / HEAD_DIM)))
    ang = np.arange(seq, dtype=f32)[:, None] * inv_freq[None, :]
    cos, sin = np.cos(ang), np.sin(ang)
    reps = LANES // HEAD_DIM
    cos_t = np.tile(np.concatenate([cos, cos], axis=1), (1, reps))
    sin_t = np.tile(np.concatenate([-sin, sin], axis=1), (1, reps))
    return jnp.asarray(cos_t, F32), jnp.asarray(sin_t, F32)


def kernel(x, mem, ln_mix_g, w_in, b_gate, w_branch_a, w_branch_b, w_out, ln_x_g, ln_mem_g, w_xq, w_xkv, w_xo,
           ln_ffn_g, w_up, conv_w, conv_b, w_down, ln_f_g):
    b, s, d = x.shape
    depth = w_in.shape[0]
    assert depth == 1, "the final norm is fused into the (single) layer's ffn kernel"
    cos_t, sin_t = _rope_tables(s)
    h = x.reshape(b * s, d)
    for l in range(depth):
        (qa0, qa1, qa2, ka0, ka1, ka2, va0, va1, va2, qb, kb, vb, ga, gb) = _in_proj(
            h, ln_mix_g[l][None], w_in[l], b_gate[l][None], cos_t, sin_t, s)
        o_groups, lse_groups = [], []
        for g, ((window, dilation), q, k, v) in enumerate(zip(
                DIL_GROUPS, (qa0, qa1, qa2), (ka0, ka1, ka2), (va0, va1, va2))):
            o, lse = _dil_attn(q, k, v, b, window // dilation, f"dil_attn_g{g}")
            o_groups.append(o)
            lse_groups.append(lse)
        sb3 = (b, s, SB_WIDTH)
        sb = _stickbreak(qb.reshape(sb3), kb.reshape(sb3), vb.reshape(sb3)).reshape(b * s, SB_WIDTH)
        kv = _mem_kv(mem, ln_mem_g[l][None], w_xkv[l].astype(BF16))
        h = _mix_xattn(o_groups, lse_groups, sb, ga, gb, h,
                       w_branch_a[l].astype(BF16), w_branch_b[l].astype(BF16), w_out[l].astype(BF16),
                       ln_x_g[l][None], w_xq[l].astype(BF16), kv, w_xo[l].astype(BF16), s)
        h = _ffn(h, ln_ffn_g[l][None], w_up[l], conv_w[l], conv_b[l][None], w_down[l], ln_f_g[None], s)
    return h.reshape(b, s, d)
```

```python
import functools

import jax
import jax.numpy as jnp
import numpy as np
from jax import lax
from jax.experimental import pallas as pl
from jax.experimental.pallas import tpu as pltpu

F32 = jnp.float32
BF16 = jnp.bfloat16

LANES = 128
SUBLANES = 8
VMEM_LIMIT_BYTES = 56 * 1024 * 1024

HEAD_DIM = 64
DIL_GROUPS = ((128, 1), (512, 4), (2048, 16))
A_HEADS_PER_GROUP = 4
A_GROUP_WIDTH = A_HEADS_PER_GROUP * HEAD_DIM
SB_HEADS = 8
SB_WIDTH = SB_HEADS * HEAD_DIM
BLOCK = 128
ROPE_THETA = 10000.0
X_HEADS = 4
X_HEAD_DIM = 128
X_WIDTH = X_HEADS * X_HEAD_DIM
CONV_WIDTH = 3
EPS = 1e-6

F32_EXP_ZERO_BELOW = -104.0
MASKED_SCORE = -1e30


def _rms_norm_bf16(x, g):
    ms = jnp.mean(x * x, axis=-1, keepdims=True)
    return (x * lax.rsqrt(ms + EPS) * g).astype(BF16)


def _dot(a, b):
    return jnp.dot(a, b, preferred_element_type=F32)


def _dot_nt(a, b):
    return lax.dot_general(a, b, (((1,), (1,)), ((), ())), preferred_element_type=F32)


def _params(*semantics):
    return pltpu.CompilerParams(dimension_semantics=semantics, vmem_limit_bytes=VMEM_LIMIT_BYTES)


def _fetch_as_bf16(src_hbm, dst_ref, stage_ref, sems, chunks):
    copies = [pltpu.make_async_copy(src_hbm.at[idx], stage_ref.at[k % 2], sems.at[k % 2])
              for k, idx in enumerate(chunks)]
    for copy in copies[:2]:
        copy.start()
    for k, idx in enumerate(chunks):
        copies[k].wait()
        dst_ref[idx] = stage_ref[k % 2].astype(BF16)
        if k + 2 < len(chunks):
            copies[k + 2].start()


def _mem_kv_kernel(m_ref, g_ref, w_ref, o_ref):
    n = _rms_norm_bf16(m_ref[0], g_ref[...])
    o_ref[0] = _dot(n, w_ref[...]).astype(BF16)


def _mem_kv(mem, g, w_bf16):
    b, ml, d = mem.shape
    n_out = w_bf16.shape[1]
    return pl.pallas_call(
        _mem_kv_kernel,
        grid=(b,),
        in_specs=[pl.BlockSpec((1, ml, d), lambda i: (i, 0, 0)),
                  pl.BlockSpec((1, d), lambda i: (0, 0)),
                  pl.BlockSpec((d, n_out), lambda i: (0, 0))],
        out_specs=pl.BlockSpec((1, ml, n_out), lambda i: (i, 0, 0)),
        out_shape=jax.ShapeDtypeStruct((b, ml, n_out), BF16),
        compiler_params=_params("arbitrary"),
        name="mem_kv",
    )(mem, g, w_bf16)


IN_PROJ_ROWS = 1024


def _in_proj_kernel(x_ref, g_ref, w_hbm, bg_ref, cos_ref, sin_ref,
                    qa0, qa1, qa2, ka0, ka1, ka2, va0, va1, va2, qb, kb, vb, ga, gb,
                    stage_ref, w_ref, w_stage, sems):
    @pl.when(pl.program_id(0) == 0)
    def _():
        w_rows = w_stage.shape[1]
        _fetch_as_bf16(w_hbm, w_ref, w_stage, sems,
                       [(pl.ds(r * w_rows, w_rows), slice(None)) for r in range(w_ref.shape[0] // w_rows)])

    n = _rms_norm_bf16(x_ref[...], g_ref[...])
    rows = n.shape[0]
    cos = cos_ref[...]
    sin = sin_ref[...]
    lane = lax.broadcasted_iota(jnp.int32, (rows, LANES), 1)
    first_half = (lane % HEAD_DIM) < (HEAD_DIM // 2)

    def rope(t):
        partner = jnp.where(first_half, pltpu.roll(t, LANES - HEAD_DIM // 2, 1),
                            pltpu.roll(t, HEAD_DIM // 2, 1))
        return t * cos + partner * sin

    def proj(c0, width):
        return _dot(n, w_ref[:, c0:c0 + width])

    def store_dilated(ref, t, dilation):
        if dilation == 1:
            ref[...] = t.astype(BF16)
            return
        n_planes = A_GROUP_WIDTH // LANES
        for c in range(n_planes):
            stage_ref[c] = t[:, c * LANES:(c + 1) * LANES]
        for r in range(dilation):
            for c in range(n_planes):
                c0 = r * A_GROUP_WIDTH + c * LANES
                ref[:, c0:c0 + LANES] = stage_ref[c, pl.ds(r, rows // dilation, stride=dilation), :].astype(BF16)

    q_scale = HEAD_DIM ** -0.5
    dilations = [d for _, d in DIL_GROUPS]
    chunk = 256
    d_model = ga.shape[1]
    a_cols = 3 * len(DIL_GROUPS) * A_GROUP_WIDTH
    sb_cols = 3 * SB_WIDTH

    def window_attention_columns():
        col = 0
        for refs, roped, scale in (((qa0, qa1, qa2), True, q_scale), ((ka0, ka1, ka2), True, None),
                                   ((va0, va1, va2), False, None)):
            for ref, dilation in zip(refs, dilations):
                t = proj(col, A_GROUP_WIDTH)
                if roped:
                    t = jnp.concatenate([rope(t[:, h * LANES:(h + 1) * LANES])
                                         for h in range(A_GROUP_WIDTH // LANES)], axis=1)
                if scale is not None:
                    t = t * scale
                store_dilated(ref, t, dilation)
                col += A_GROUP_WIDTH

    def stickbreak_columns():
        col = a_cols
        for ref, scale in ((qb, q_scale), (kb, None), (vb, None)):
            for c in range(SB_WIDTH // chunk):
                t = proj(col, chunk)
                if scale is not None:
                    t = t * scale
                ref[:, c * chunk:(c + 1) * chunk] = t.astype(BF16)
                col += chunk

    def gate_columns():
        col, gcol = a_cols + sb_cols, 0
        for ref in (ga, gb):
            for c in range(d_model // chunk):
                t = proj(col, chunk) + bg_ref[:, gcol:gcol + chunk]
                ref[:, c * chunk:(c + 1) * chunk] = (1.0 / (1.0 + jnp.exp(-t))).astype(BF16)
                col += chunk
                gcol += chunk

    gate_columns()
    window_attention_columns()
    stickbreak_columns()


IN_PROJ_WEIGHT_CHUNK = 64


def _in_proj(x2, g, w_in, b_gate, cos_t, sin_t, seq):
    t_rows, d = x2.shape
    tm = IN_PROJ_ROWS
    n_cols = w_in.shape[1]
    assert d % IN_PROJ_WEIGHT_CHUNK == 0
    tiles_per_seq = seq // tm
    dilations = [dil for _, dil in DIL_GROUPS]
    outs = [(dil, dil * A_GROUP_WIDTH) for dil in dilations] * 3 + [(1, SB_WIDTH)] * 3 + [(1, d)] * 2
    return pl.pallas_call(
        _in_proj_kernel,
        grid=(t_rows // tm,),
        in_specs=[pl.BlockSpec((tm, d), lambda i: (i, 0)),
                  pl.BlockSpec((1, d), lambda i: (0, 0)),
                  pl.BlockSpec(memory_space=pl.ANY),
                  pl.BlockSpec((1, 2 * d), lambda i: (0, 0)),
                  pl.BlockSpec((tm, LANES), lambda i: (i % tiles_per_seq, 0)),
                  pl.BlockSpec((tm, LANES), lambda i: (i % tiles_per_seq, 0))],
        out_specs=[pl.BlockSpec((tm // div, w), lambda i: (i, 0)) for div, w in outs],
        out_shape=[jax.ShapeDtypeStruct((t_rows // div, w), BF16) for div, w in outs],
        scratch_shapes=[pltpu.VMEM((A_GROUP_WIDTH // LANES, tm, LANES), F32),
                        pltpu.VMEM((d, n_cols), BF16),
                        pltpu.VMEM((2, IN_PROJ_WEIGHT_CHUNK, n_cols), F32),
                        pltpu.SemaphoreType.DMA((2,))],
        compiler_params=_params("arbitrary"),
        name="in_proj",
    )(x2, g, w_in, b_gate, cos_t, sin_t)


def _dil_attn_kernel(q_ref, kc_ref, kp_ref, vc_ref, vp_ref, o_ref, lse_ref, *, n_back):
    j = pl.program_id(2)
    rows, width = q_ref.shape[1], q_ref.shape[2]
    row = lax.broadcasted_iota(jnp.int32, (2 * BLOCK, 2 * BLOCK), 0) % BLOCK
    col = lax.broadcasted_iota(jnp.int32, (2 * BLOCK, 2 * BLOCK), 1)
    dist = row - col + BLOCK
    band = (dist >= 0) & (dist <= n_back)
    first_col = jnp.where(j > 0, 0, BLOCK)
    band_first = band & (col >= first_col)
    lane = lax.broadcasted_iota(jnp.int32, (BLOCK, LANES), 1)
    head0 = lane < HEAD_DIM

    for cb in range(width // LANES):
        lanes = slice(cb * LANES, (cb + 1) * LANES)
        for sub in range(rows // BLOCK):
            cur = slice(sub * BLOCK, (sub + 1) * BLOCK)
            q = q_ref[0, cur, lanes]
            if sub == 0:
                k2 = jnp.concatenate([kp_ref[0, :, lanes], kc_ref[0, cur, lanes]], axis=0)
                v2 = jnp.concatenate([vp_ref[0, :, lanes], vc_ref[0, cur, lanes]], axis=0)
                mask = band_first
            else:
                both = slice((sub - 1) * BLOCK, (sub + 1) * BLOCK)
                k2 = kc_ref[0, both, lanes]
                v2 = vc_ref[0, both, lanes]
                mask = band
            zero = jnp.zeros_like(q)
            q2 = jnp.concatenate([jnp.where(head0, q, zero), jnp.where(head0, zero, q)], axis=0)
            s = jnp.where(mask, _dot_nt(q2, k2), MASKED_SCORE)
            m = jnp.max(s, axis=1, keepdims=True)
            p = jnp.exp(s - m)
            l = jnp.sum(p, axis=1, keepdims=True)
            o2 = _dot(p.astype(BF16), v2) * (1.0 / l)
            lse2 = m + jnp.log(l)
            o_ref[0, cur, lanes] = jnp.where(head0, o2[:BLOCK], o2[BLOCK:]).astype(o_ref.dtype)
            lse_ref[0, cur, lanes] = jnp.where(head0, lse2[:BLOCK], lse2[BLOCK:])


DIL_ATTN_BLOCKS_PER_STEP = 32


def _dil_attn(q, k, v, batch, n_back, name):
    assert n_back <= BLOCK
    total_rows, width = q.shape
    length = total_rows // batch
    chunk_width = min(width, 4 * LANES)
    rows = min(length, BLOCK * DIL_ATTN_BLOCKS_PER_STEP * LANES // chunk_width)
    blocks_per_chunk = rows // BLOCK
    chunk_width = min(width, LANES * DIL_ATTN_BLOCKS_PER_STEP // blocks_per_chunk)
    view = lambda t: t.reshape(batch, length, width)
    cur_spec = pl.BlockSpec((1, rows, chunk_width), lambda bi, c, j: (bi, j, c))
    prev_spec = pl.BlockSpec((1, BLOCK, chunk_width),
                             lambda bi, c, j: (bi, jnp.maximum(j * blocks_per_chunk - 1, 0), c))
    o, lse = pl.pallas_call(
        functools.partial(_dil_attn_kernel, n_back=n_back),
        grid=(batch, width // chunk_width, length // rows),
        in_specs=[cur_spec, cur_spec, prev_spec, cur_spec, prev_spec],
        out_specs=[cur_spec, cur_spec],
        out_shape=[jax.ShapeDtypeStruct((batch, length, width), BF16),
                   jax.ShapeDtypeStruct((batch, length, width), F32)],
        compiler_params=_params("arbitrary", "arbitrary", "arbitrary"),
        name=name,
    )(view(q), view(k), view(k), view(v), view(v))
    return o.reshape(total_rows, width), lse.reshape(total_rows, width)


SB_FIRST_BLOCKS = 3
SB_TOP_ROWS = 32
SB_GROUP_HEADS = 4
SB_GROUP_WIDTH = SB_GROUP_HEADS * HEAD_DIM


def _suffix_sum_matrix(n_keys):
    j = np.arange(n_keys)[:, None]
    s = np.arange(n_keys)[None, :]
    u = (j > s).astype(np.float32)
    return jnp.asarray(np.concatenate([u, u], axis=0), BF16)


def _stickbreak_kernel(q_ref, k_ref, v_ref, u_ref, o_ref, c_ref, acc_ref):
    step = pl.program_id(1)
    n_sub = q_ref.shape[1] // BLOCK
    n_groups = q_ref.shape[2] // SB_GROUP_WIDTH
    group_rows = SB_GROUP_HEADS * BLOCK
    group_cols = [slice(g * SB_GROUP_WIDTH, (g + 1) * SB_GROUP_WIDTH) for g in range(n_groups)]
    head_of_lane = lax.broadcasted_iota(jnp.int32, (BLOCK, SB_GROUP_WIDTH), 1) // HEAD_DIM

    def stack_heads(q):
        zero = jnp.zeros_like(q)
        return jnp.concatenate([jnp.where(head_of_lane == h, q, zero) for h in range(SB_GROUP_HEADS)], axis=0)

    q_stacked = {(sub, g): stack_heads(q_ref[0, sub * BLOCK:(sub + 1) * BLOCK, group_cols[g]])
                 for sub in range(n_sub) for g in range(n_groups)}
    def head_rows(g, r0, nr):
        return [slice(g * group_rows + h * BLOCK + r0, g * group_rows + h * BLOCK + r0 + nr)
                for h in range(SB_GROUP_HEADS)]

    def windows(jobs):
        units = [job + (g,) for job in jobs for g in range(n_groups)]

        scores = []
        for sub, k0, n_blocks, mask_from, r0, nr, fresh, g in units:
            qs = q_stacked[sub, g]
            if nr != BLOCK:
                qs = jnp.concatenate([qs[h * BLOCK + r0:h * BLOCK + r0 + nr] for h in range(SB_GROUP_HEADS)], axis=0)
            scores.append(_dot_nt(qs, k_ref[0, pl.ds(k0, n_blocks * BLOCK), group_cols[g]]))
        stage_b = []
        for (sub, k0, n_blocks, mask_from, r0, nr, fresh, g), z_all in zip(units, scores):
            if mask_from < n_blocks:
                row_iota = lax.broadcasted_iota(jnp.int32, (nr, BLOCK), 0)
                q_pos = (step * n_sub + sub) * BLOCK + r0 + jnp.concatenate([row_iota] * SB_GROUP_HEADS, axis=0)
                key_in_block = lax.broadcasted_iota(jnp.int32, (SB_GROUP_HEADS * nr, BLOCK), 1)
            log_betas, split, later = {}, {}, {}
            run = None
            for j in reversed(range(n_blocks)):
                z = z_all[:, j * BLOCK:(j + 1) * BLOCK]
                cost = jnp.maximum(z, 0.0) + jnp.log(1.0 + jnp.exp(-jnp.abs(z)))
                log_beta = z - cost
                if j >= mask_from:
                    valid = (k0 + j * BLOCK + key_in_block) < q_pos
                    cost = jnp.where(valid, cost, 0.0)
                    log_beta = jnp.where(valid, log_beta, MASKED_SCORE)
                hi = cost.astype(BF16)
                lo = (cost - hi.astype(F32)).astype(BF16)
                log_betas[j] = log_beta
                split[j] = jnp.concatenate([hi, lo], axis=1)
                later[j] = run
                tot = jnp.sum(cost, axis=1, keepdims=True)
                run = tot if run is None else run + tot
            within = _dot(jnp.concatenate([split[j] for j in range(n_blocks)], axis=0), u_ref[...])
            stage_b.append((log_betas, later, run, within))
        for (sub, k0, n_blocks, mask_from, r0, nr, fresh, g), (log_betas, later, total, within) in zip(units, stage_b):
            unit_rows = SB_GROUP_HEADS * nr
            if fresh:
                c_old = None
            else:
                c_old = jnp.concatenate([c_ref[sub, rows, :] for rows in head_rows(g, r0, nr)], axis=0)
            weights = []
            for j in range(n_blocks):
                cost_after = within[j * unit_rows:(j + 1) * unit_rows]
                if later[j] is not None:
                    cost_after = cost_after + later[j]
                if c_old is not None:
                    cost_after = cost_after + c_old
                weights.append(jnp.exp(log_betas[j] - cost_after).astype(BF16))
            c_new = jnp.broadcast_to(total, (unit_rows, LANES)) if fresh else c_old + total
            for h, rows in enumerate(head_rows(g, r0, nr)):
                c_ref[sub, rows, :] = c_new[h * nr:(h + 1) * nr]
            pv = _dot(jnp.concatenate(weights, axis=1),
                      v_ref[0, pl.ds(k0, n_blocks * BLOCK), group_cols[g]])
            lane_head = lax.broadcasted_iota(jnp.int32, (nr, SB_GROUP_WIDTH), 1) // HEAD_DIM
            picked = pv[(SB_GROUP_HEADS - 1) * nr:]
            for h in reversed(range(SB_GROUP_HEADS - 1)):
                picked = jnp.where(lane_head == h, pv[h * nr:(h + 1) * nr], picked)
            if fresh:
                acc_ref[sub, r0:r0 + nr, group_cols[g]] = picked
            else:
                acc_ref[sub, r0:r0 + nr, group_cols[g]] += picked

    def all_dead(sub, r0=0, nr=BLOCK):
        least = None
        for g in range(n_groups):
            for sl in head_rows(g, r0, nr):
                least = c_ref[sub, sl, :] if least is None else jnp.minimum(least, c_ref[sub, sl, :])
        return jnp.min(least) > -F32_EXP_ZERO_BELOW

    qi0 = step * n_sub
    top = SB_TOP_ROWS

    near = SB_FIRST_BLOCKS - 1

    def key_start(block):
        return block * BLOCK if isinstance(block, int) else pl.multiple_of(block * BLOCK, BLOCK)

    def run_blocks(query_blocks):
        first_jobs, top_jobs, walkers = [], [], []
        for sub, qb in enumerate(query_blocks):
            if isinstance(qb, int) and qb < near:
                first_jobs.append((sub, 0, qb + 1, qb, 0, BLOCK, True))
            else:
                first_jobs.append((sub, key_start(qb - near + 1), near, near - 1, 0, BLOCK, True))
                top_jobs.append((sub, key_start(qb - near), 1, 1, 0, top, False))
                walkers.append((sub, qb - near))
        windows(sorted(first_jobs + top_jobs, key=lambda job: (job[0], not job[6])))
        top_dead = {sub: all_dead(sub, 0, top) for sub, _ in walkers}
        rest_dead = {sub: all_dead(sub, top, BLOCK - top) for sub, _ in walkers}
        for sub, far_block in walkers:
            @pl.when(jnp.logical_not(rest_dead[sub]))
            def _(sub=sub, far_block=far_block):
                windows([(sub, key_start(far_block), 1, 1, top, BLOCK - top, False)])

            if isinstance(far_block, int) and far_block == 0:
                continue

            def body(carry, sub=sub):
                kb, _ = carry
                windows([(sub, key_start(kb), 1, 1, 0, BLOCK, False)])
                done = jnp.logical_or(kb == 0, all_dead(sub))
                return kb - 1, done.astype(jnp.int32)

            done_first = jnp.logical_or(far_block == 0, jnp.logical_and(top_dead[sub], rest_dead[sub]))
            lax.while_loop(lambda c: c[1] == 0, body,
                           (jnp.asarray(far_block - 1, jnp.int32), done_first.astype(jnp.int32)))

    @pl.when(step > 0)
    def _():
        run_blocks([qi0 + sub for sub in range(n_sub)])

    @pl.when(step == 0)
    def _():
        run_blocks(list(range(n_sub)))

    for sub in range(n_sub):
        o_ref[0, sub * BLOCK:(sub + 1) * BLOCK, :] = acc_ref[sub].astype(o_ref.dtype)


SB_QUERY_BLOCKS_PER_STEP = 4


def _stickbreak(q, k, v):
    b, s, w = q.shape
    n_sub = SB_QUERY_BLOCKS_PER_STEP
    assert n_sub >= SB_FIRST_BLOCKS - 1 and s % (n_sub * BLOCK) == 0
    assert s >= SB_FIRST_BLOCKS * BLOCK and w % SB_GROUP_WIDTH == 0
    u = _suffix_sum_matrix(BLOCK)
    blk = pl.BlockSpec((1, n_sub * BLOCK, w), lambda bi, qi: (bi, qi, 0))
    full = pl.BlockSpec((1, s, w), lambda bi, qi: (bi, 0, 0))
    const = lambda arr: pl.BlockSpec(arr.shape, lambda bi, qi: (0, 0))
    n_heads = w // HEAD_DIM
    return pl.pallas_call(
        _stickbreak_kernel,
        grid=(b, s // (n_sub * BLOCK)),
        in_specs=[blk, full, full, const(u)],
        out_specs=blk,
        out_shape=jax.ShapeDtypeStruct((b, s, w), BF16),
        scratch_shapes=[pltpu.VMEM((n_sub, n_heads * BLOCK, LANES), F32), pltpu.VMEM((n_sub, BLOCK, w), F32)],
        compiler_params=_params("arbitrary", "arbitrary"),
        name="stickbreak",
    )(q, k, v, u)


MIX_ROWS = 1024
MIX_ROW_PARTS = 4


def _mix_xattn_kernel(o0, o1, o2, l0, l1, l2, sb, ga, gb, x_ref, wa, wb, wo, gx, wxq, kv, wxo, out_ref,
                      so1, sl1, so2, sl2):
    rows = x_ref.shape[0]

    def undilate(ref, stage_ref, dilation):
        n_planes = A_GROUP_WIDTH // LANES
        for r in range(dilation):
            for c in range(n_planes):
                c0 = r * A_GROUP_WIDTH + c * LANES
                stage_ref[c, pl.ds(r, rows // dilation, stride=dilation), :] = ref[:, c0:c0 + LANES].astype(F32)
        return jnp.concatenate([stage_ref[c] for c in range(n_planes)], axis=1)

    dilations = [d for _, d in DIL_GROUPS]
    assert dilations[0] == 1
    ob, lb = undilate(o1, so1, dilations[1]), undilate(l1, sl1, dilations[1])
    oc, lc = undilate(o2, so2, dilations[2]), undilate(l2, sl2, dilations[2])

    part_rows = rows // MIX_ROW_PARTS
    parts = [slice(p * part_rows, (p + 1) * part_rows) for p in range(MIX_ROW_PARTS)]

    def mixed_residual(r):
        yb = _dot(sb[r, :], wb[...])
        la = l0[r, :]
        m = jnp.maximum(jnp.maximum(la, lb[r]), lc[r])
        ea, eb, ec = jnp.exp(la - m), jnp.exp(lb[r] - m), jnp.exp(lc[r] - m)
        y = (ea * o0[r, :].astype(F32) + eb * ob[r] + ec * oc[r]) * (1.0 / (ea + eb + ec))
        ya = _dot(y.astype(BF16), wa[...])
        mix = ga[r, :].astype(F32) * ya + gb[r, :].astype(F32) * yb
        return x_ref[r, :] + _dot(mix.astype(BF16), wo[...])

    def cross_attention(q):
        scale = X_HEAD_DIM ** -0.5
        heads = []
        for hd in range(X_HEADS):
            cols = slice(hd * X_HEAD_DIM, (hd + 1) * X_HEAD_DIM)
            kh = kv[0, :, cols]
            vh = kv[0, :, X_WIDTH + hd * X_HEAD_DIM:X_WIDTH + (hd + 1) * X_HEAD_DIM]
            s = _dot_nt(q[:, cols].astype(BF16), kh) * scale
            mx = jnp.max(s, axis=1, keepdims=True)
            p = jnp.exp(s - mx)
            l = jnp.sum(p, axis=1, keepdims=True)
            heads.append((_dot(p.astype(BF16), vh) * (1.0 / l)).astype(BF16))
        return jnp.concatenate(heads, axis=1)

    hs = [mixed_residual(r) for r in parts]
    qs = [_dot(_rms_norm_bf16(h, gx[...]), wxq[...]) for h in hs]
    os_ = [cross_attention(q) for q in qs]
    for r, h, o in zip(parts, hs, os_):
        out_ref[r, :] = h + _dot(o, wxo[...])


def _mix_xattn(o_groups, lse_groups, sb, ga, gb, x2, wa, wb, wo, gx, wxq, kv, wxo, seq):
    t_rows, d = x2.shape
    tm = MIX_ROWS
    tiles_per_seq = seq // tm
    row = lambda width: pl.BlockSpec((tm, width), lambda i: (i, 0))
    dil = lambda dilation: pl.BlockSpec((tm // dilation, dilation * A_GROUP_WIDTH), lambda i: (i, 0))
    const = lambda arr: pl.BlockSpec(arr.shape, lambda i: (0,) * arr.ndim)
    ml, kvw = kv.shape[1], kv.shape[2]
    group_specs = [dil(dilation) for _, dilation in DIL_GROUPS]
    return pl.pallas_call(
        _mix_xattn_kernel,
        grid=(t_rows // tm,),
        in_specs=group_specs + group_specs + [row(SB_WIDTH), row(d), row(d), row(d),
                  const(wa), const(wb), const(wo), const(gx), const(wxq),
                  pl.BlockSpec((1, ml, kvw), lambda i: (i // tiles_per_seq, 0, 0)),
                  const(wxo)],
        out_specs=row(d),
        out_shape=jax.ShapeDtypeStruct((t_rows, d), F32),
        scratch_shapes=[pltpu.VMEM((A_GROUP_WIDTH // LANES, tm, LANES), F32)] * 4,
        compiler_params=_params("arbitrary"),
        name="mix_xattn",
    )(*o_groups, *lse_groups, sb, ga, gb, x2, wa, wb, wo, gx, wxq, kv, wxo)


FFN_ROWS = 512
FFN_CHUNK = 256
HALO = SUBLANES


def _ffn_kernel(h_ref, g_ref, wup_hbm, cw_ref, cb_ref, wdn_hbm, gf_ref, out_ref,
                halo_ref, u_ref, act_ref, wup, wdn, up_stage, dn_stage, sems, *, tiles_per_seq):
    i = pl.program_id(0)
    rows = h_ref.shape[0]
    d_ff = wdn.shape[0]

    @pl.when(i == 0)
    def _():
        up_rows = up_stage.shape[1]
        _fetch_as_bf16(wup_hbm, wup, up_stage, sems,
                       [(pl.ds(r * up_rows, up_rows), slice(None)) for r in range(wup.shape[0] // up_rows)])
        dn_rows = dn_stage.shape[1]
        _fetch_as_bf16(wdn_hbm, wdn, dn_stage, sems,
                       [(pl.ds(r * dn_rows, dn_rows), slice(None)) for r in range(d_ff // dn_rows)])

    @pl.when(i % tiles_per_seq == 0)
    def _():
        halo_ref[...] = jnp.zeros_like(halo_ref)

    h = h_ref[...]
    n = _rms_norm_bf16(h, g_ref[...])

    def conv(c0):
        cols = slice(c0, c0 + FFN_CHUNK)
        u = _dot(n, wup[:, cols])
        u_ref[0:HALO, :] = halo_ref[:, cols]
        u_ref[HALO:HALO + rows, :] = u
        halo_ref[:, cols] = u[rows - HALO:, :]
        out = cb_ref[:, cols] + cw_ref[CONV_WIDTH - 1:CONV_WIDTH, cols] * u
        for back in range(1, CONV_WIDTH):
            tap = CONV_WIDTH - 1 - back
            out = out + cw_ref[tap:tap + 1, cols] * u_ref[HALO - back:HALO - back + rows, :]
        return out

    for c in range(d_ff // FFN_CHUNK):
        gate = conv(c * FFN_CHUNK)
        val = conv(d_ff + c * FFN_CHUNK)
        act = gate * (1.0 / (1.0 + jnp.exp(-gate))) * val
        act_ref[:, c * FFN_CHUNK:(c + 1) * FFN_CHUNK] = act.astype(BF16)

    y = h + _dot(act_ref[...], wdn[...])
    ms = jnp.mean(y * y, axis=-1, keepdims=True)
    out_ref[...] = y * lax.rsqrt(ms + EPS) * gf_ref[...]


FFN_UP_WEIGHT_CHUNK = 128
FFN_DOWN_WEIGHT_CHUNK = 256


def _ffn(h2, g, wup, conv_w, conv_b, wdn, gf, seq):
    t_rows, d = h2.shape
    tm = FFN_ROWS
    d_ff = wdn.shape[0]
    assert d_ff % FFN_CHUNK == 0 and seq % tm == 0
    assert wup.shape[0] % FFN_UP_WEIGHT_CHUNK == 0 and d_ff % FFN_DOWN_WEIGHT_CHUNK == 0
    const = lambda arr: pl.BlockSpec(arr.shape, lambda i: (0,) * arr.ndim)
    in_hbm = pl.BlockSpec(memory_space=pl.ANY)
    return pl.pallas_call(
        functools.partial(_ffn_kernel, tiles_per_seq=seq // tm),
        grid=(t_rows // tm,),
        in_specs=[pl.BlockSpec((tm, d), lambda i: (i, 0)),
                  const(g), in_hbm, const(conv_w), const(conv_b), in_hbm, const(gf)],
        out_specs=pl.BlockSpec((tm, d), lambda i: (i, 0)),
        out_shape=jax.ShapeDtypeStruct((t_rows, d), F32),
        scratch_shapes=[pltpu.VMEM((HALO, 2 * d_ff), F32),
                        pltpu.VMEM((HALO + tm, FFN_CHUNK), F32),
                        pltpu.VMEM((tm, d_ff), BF16),
                        pltpu.VMEM(wup.shape, BF16),
                        pltpu.VMEM(wdn.shape, BF16),
                        pltpu.VMEM((2, FFN_UP_WEIGHT_CHUNK, wup.shape[1]), F32),
                        pltpu.VMEM((2, FFN_DOWN_WEIGHT_CHUNK, wdn.shape[1]), F32),
                        pltpu.SemaphoreType.DMA((2,))],
        compiler_params=_params("arbitrary"),
        name="ffn",
    )(h2, g, wup, conv_w, conv_b, wdn, gf)


def _rope_tables(seq):
    half = HEAD_DIM // 2
    f32 = np.float32
    inv_freq = f32(1.0) / (f32(ROPE_THETA) ** (np.arange(half, dtype=f32) * f32(2.0 / HEAD_DIM)))
    ang = np.arange(seq, dtype=f32)[:, None] * inv_freq[None, :]
    cos, sin = np.cos(ang), np.sin(ang)
    reps = LANES // HEAD_DIM
    cos_t = np.tile(np.concatenate([cos, cos], axis=1), (1, reps))
    sin_t = np.tile(np.concatenate([-sin, sin], axis=1), (1, reps))
    return jnp.asarray(cos_t, F32), jnp.asarray(sin_t, F32)


def kernel(x, mem, ln_mix_g, w_in, b_gate, w_branch_a, w_branch_b, w_out, ln_x_g, ln_mem_g, w_xq, w_xkv, w_xo,
           ln_ffn_g, w_up, conv_w, conv_b, w_down, ln_f_g):
    b, s, d = x.shape
    depth = w_in.shape[0]
    assert depth == 1, "the final norm is fused into the (single) layer's ffn kernel"
    cos_t, sin_t = _rope_tables(s)
    h = x.reshape(b * s, d)
    for l in range(depth):
        (qa0, qa1, qa2, ka0, ka1, ka2, va0, va1, va2, qb, kb, vb, ga, gb) = _in_proj(
            h, ln_mix_g[l][None], w_in[l], b_gate[l][None], cos_t, sin_t, s)
        o_groups, lse_groups = [], []
        for g, ((window, dilation), q, k, v) in enumerate(zip(
                DIL_GROUPS, (qa0, qa1, qa2), (ka0, ka1, ka2), (va0, va1, va2))):
            o, lse = _dil_attn(q, k, v, b, window // dilation, f"dil_attn_g{g}")
            o_groups.append(o)
            lse_groups.append(lse)
        sb3 = (b, s, SB_WIDTH)
        sb = _stickbreak(qb.reshape(sb3), kb.reshape(sb3), vb.reshape(sb3)).reshape(b * s, SB_WIDTH)
        kv = _mem_kv(mem, ln_mem_g[l][None], w_xkv[l].astype(BF16))
        h = _mix_xattn(o_groups, lse_groups, sb, ga, gb, h,
                       w_branch_a[l].astype(BF16), w_branch_b[l].astype(BF16), w_out[l].astype(BF16),
                       ln_x_g[l][None], w_xq[l].astype(BF16), kv, w_xo[l].astype(BF16), s)
        h = _ffn(h, ln_ffn_g[l][None], w_up[l], conv_w[l], conv_b[l][None], w_down[l], ln_f_g[None], s)
    return h.reshape(b, s, d)
```
